```python
import math
import jax
import jax.numpy as jnp
from jax import lax
import numpy as np

D_MODEL = 1024
BATCH = 8
SEQ = 2048
DEPTH = 4

D_MIX = D_MODEL
HY_WIDTH = D_MIX // 4
LRU_WIDTH = D_MIX // 4
MLA_HEADS = 8
MLA_V_DIM = (D_MIX - HY_WIDTH - LRU_WIDTH) // MLA_HEADS
MLA_NOPE_DIM = MLA_V_DIM
MLA_ROPE_DIM = MLA_V_DIM // 2
MLA_KV_LORA = 4 * MLA_V_DIM
MLA_Q_LORA = 12 * MLA_V_DIM
ROPE_THETA = 10000.0
Q_BLOCK = 128
HY_ORDER = 2
HY_SHORT_CONV = 3
HY_POS_EMB = 33
HY_FILTER_FFN = 64
HY_FAST_DECAY = 0.3
HY_SLOW_DECAY = 1.5
HY_DECAY_TARGET = 1e-2
LRU_BLOCKS = 4
LRU_BLOCK_WIDTH = LRU_WIDTH // LRU_BLOCKS
LRU_CONV = 4
LRU_C = 8.0
N_EXPERTS = 16
EC_CAPACITY_FACTOR = 2
EXPERT_FF = D_MODEL
EPS = 1e-5
IN_SIZES = [(HY_ORDER + 1) * HY_WIDTH, LRU_WIDTH, LRU_WIDTH, MLA_Q_LORA, MLA_KV_LORA, MLA_ROPE_DIM]
N_IN = sum(IN_SIZES)
IN_SPLITS = [int(v) for v in np.cumsum(IN_SIZES)[:-1]]

kernel_name = "hybrid_hyena_rglru_mla_ecmoe_encoder"

F32 = jnp.float32


def _layer_norm(x, g, b):
    xf = x.astype(F32)
    xc = xf - jnp.mean(xf, axis=-1, keepdims=True)
    var = jnp.mean(xc * xc, axis=-1, keepdims=True)
    return (xc * lax.rsqrt(var + EPS) * g.astype(F32) + b.astype(F32)).astype(x.dtype)


def _rms_norm(x, g):
    xf = x.astype(F32)
    y = xf * lax.rsqrt(jnp.mean(xf * xf, axis=-1, keepdims=True) + EPS)
    return (y * g.astype(F32)).astype(x.dtype)


def _depthwise_conv(x, w, b, pad_left, pad_right):
    c = x.shape[-1]
    y = lax.conv_general_dilated(x, w[:, None, :].astype(x.dtype), window_strides=(1,),
                                 padding=[(pad_left, pad_right)],
                                 dimension_numbers=("NWC", "WIO", "NWC"),
                                 feature_group_count=c)
    return y + b.astype(x.dtype)


def _hyena_filters(length, w1, b1, f1, w2, b2, f2, w3):
    t = jnp.linspace(0.0, 1.0, length, dtype=F32)[:, None]
    bands = (HY_POS_EMB - 1) // 2
    t_idx = jnp.arange(length, dtype=F32)[:, None]
    freqs = jnp.linspace(1e-4, bands - 1, bands, dtype=F32)[None, :]
    ang = 2.0 * math.pi * t_idx * freqs / length
    z = jnp.concatenate([t, jnp.cos(ang), -jnp.sin(ang)], axis=-1)
    h = jnp.sin(f1.astype(F32) * (z @ w1.astype(F32) + b1.astype(F32)))
    h = jnp.sin(f2.astype(F32) * (h @ w2.astype(F32) + b2.astype(F32)))
    h = (h @ w3.astype(F32)).reshape(length, 2, HY_ORDER, HY_WIDTH)
    max_decay = math.log(HY_DECAY_TARGET) / HY_FAST_DECAY
    min_decay = math.log(HY_DECAY_TARGET) / HY_SLOW_DECAY
    deltas = jnp.abs(jnp.linspace(min_decay, max_decay, HY_WIDTH, dtype=F32))
    window = jnp.exp(-t * deltas[None, :])
    return h * window[:, None, None, :]


def _bidir_fftconv(u, h_fwd, h_bwd):
    length, c = h_fwd.shape
    k = jnp.concatenate([h_fwd, jnp.zeros((1, c), F32), h_bwd[1:][::-1]], axis=0)
    kf = jnp.fft.rfft(k, axis=0)
    uf = jnp.fft.rfft(u, n=2 * length, axis=1)
    return jnp.fft.irfft(uf * kf[None], n=2 * length, axis=1)[:, :length]


def _hyena_mixer(u, conv_w, conv_b, w1, b1, f1, w2, b2, f2, w3, skip):
    u = _depthwise_conv(u, conv_w, conv_b, HY_SHORT_CONV // 2, HY_SHORT_CONV // 2)
    x1, x2, v = jnp.split(u.astype(F32), HY_ORDER + 1, axis=-1)
    h = _hyena_filters(u.shape[1], w1, b1, f1, w2, b2, f2, w3)
    z = v
    for o, gate in enumerate((x1, x2)):
        z = gate * (_bidir_fftconv(z, h[:, 0, o], h[:, 1, o]) + skip[o].astype(F32) * z)
    return z.astype(u.dtype)


def _linear_combine(left, right):
    a_l, b_l = left
    a_r, b_r = right
    return a_l * a_r, a_r * b_l + b_r


def _rglru_mixer(xr, xg, conv_w, conv_b, wa, ba, wx, bx, lam):
    xr = _depthwise_conv(xr, conv_w, conv_b, LRU_CONV // 2, LRU_CONV - 1 - LRU_CONV // 2)
    bsz, seq, _ = xr.shape
    xf = xr.astype(F32)
    xb = xf.reshape(bsz, seq, LRU_BLOCKS, LRU_BLOCK_WIDTH)
    gate_a = jax.nn.sigmoid(jnp.einsum("bsnj,dnjk->dbsnk", xb, wa.astype(F32)).reshape(2, bsz, seq, LRU_WIDTH)
                            + ba.astype(F32)[:, None, None, :])
    gate_x = jax.nn.sigmoid(jnp.einsum("bsnj,dnjk->dbsnk", xb, wx.astype(F32)).reshape(2, bsz, seq, LRU_WIDTH)
                            + bx.astype(F32)[:, None, None, :])
    log_a = -LRU_C * gate_a * jax.nn.softplus(-lam.astype(F32))[:, None, None, :]
    a = jnp.exp(log_a)
    u = jnp.sqrt(-jnp.expm1(2.0 * log_a)) * gate_x * xf[None]
    _, h_fwd = lax.associative_scan(_linear_combine, (a[0], u[0]), axis=1)
    _, h_bwd = lax.associative_scan(_linear_combine, (a[1], u[1]), axis=1, reverse=True)
    return ((h_fwd + h_bwd) * jax.nn.gelu(xg.astype(F32))).astype(xr.dtype)


def _rope_tables(seq):
    inv = ROPE_THETA ** (-jnp.arange(0, MLA_ROPE_DIM, 2, dtype=F32) / MLA_ROPE_DIM)
    ang = jnp.arange(seq, dtype=F32)[:, None] * inv[None, :]
    return jnp.cos(ang), jnp.sin(ang)


def _apply_rope(x, cos, sin):
    xf = x.astype(F32)
    x1, x2 = jnp.split(xf, 2, axis=-1)
    return jnp.concatenate([x1 * cos - x2 * sin, x1 * sin + x2 * cos], axis=-1).astype(x.dtype)


def _mla_mixer(cq, ckv, k_rope, q_norm_g, w_uq, kv_norm_g, w_ukv, cos, sin):
    bsz, seq, _ = cq.shape
    q = (_rms_norm(cq, q_norm_g) @ w_uq).reshape(bsz, seq, MLA_HEADS, MLA_NOPE_DIM + MLA_ROPE_DIM)
    q_nope = q[..., :MLA_NOPE_DIM]
    q_rope = _apply_rope(q[..., MLA_NOPE_DIM:], cos[:, None, :], sin[:, None, :])
    kv = (_rms_norm(ckv, kv_norm_g) @ w_ukv).reshape(bsz, seq, MLA_HEADS, MLA_NOPE_DIM + MLA_V_DIM)
    k_nope = kv[..., :MLA_NOPE_DIM]
    v = kv[..., MLA_NOPE_DIM:]
    k_rope = _apply_rope(k_rope, cos, sin)
    nblk = seq // Q_BLOCK
    scale = (MLA_NOPE_DIM + MLA_ROPE_DIM) ** -0.5

    def to_blocks(t):
        return t.reshape(bsz, nblk, Q_BLOCK, MLA_HEADS, t.shape[-1]).transpose(1, 0, 2, 3, 4)

    def attend(blk):
        qn, qr = blk
        s = jnp.einsum("bqhd,bkhd->bhqk", qn, k_nope) + jnp.einsum("bqhr,bkr->bhqk", qr, k_rope)
        p = jax.nn.softmax(s.astype(F32) * scale, axis=-1).astype(v.dtype)
        return jnp.einsum("bhqk,bkhd->bqhd", p, v)

    o = lax.map(attend, (to_blocks(q_nope), to_blocks(q_rope)))
    return o.transpose(1, 0, 2, 3, 4).reshape(bsz, seq, MLA_HEADS * MLA_V_DIM)


def _expert_choice_moe(x, router_w, w_gate, w_up, w_down):
    bsz, seq, d = x.shape
    cap = max(1, EC_CAPACITY_FACTOR * seq // N_EXPERTS)
    affinity = jax.nn.softmax(jnp.einsum("bsd,de->bse", x, router_w).astype(F32), axis=-1)
    gate, idx = lax.top_k(jnp.swapaxes(affinity, 1, 2), cap)
    xe = jax.vmap(lambda xb, ib: xb[ib])(x, idx)
    hid = jax.nn.silu(jnp.einsum("becd,edf->becf", xe, w_gate)) * jnp.einsum("becd,edf->becf", xe, w_up)
    ye = jnp.einsum("becf,efd->becd", hid, w_down) * gate[..., None].astype(x.dtype)
    return jax.vmap(lambda ib, yb: jnp.zeros((seq, d), yb.dtype).at[ib.reshape(-1)].add(yb.reshape(-1, d)))(idx, ye)


def setup_inputs(seed: int = 0) -> dict:
    key = jax.random.key(seed)
    ks = iter(jax.random.split(key, 40))
    L = DEPTH
    beta = (8.0 * DEPTH) ** -0.25

    def nrm(shape, scale):
        return scale * jax.random.normal(next(ks), shape, F32)

    x = nrm((BATCH, SEQ, D_MODEL), 1.0)
    ln_in_g = 1.0 + nrm((D_MODEL,), 0.02)
    ln_in_b = nrm((D_MODEL,), 0.02)
    w_in = nrm((L, D_MODEL, N_IN), D_MODEL ** -0.5)
    hy_conv_w = nrm((L, HY_SHORT_CONV, (HY_ORDER + 1) * HY_WIDTH), HY_SHORT_CONV ** -0.5)
    hy_conv_b = nrm((L, (HY_ORDER + 1) * HY_WIDTH), 0.02)
    hy_ffn_w1 = nrm((L, HY_POS_EMB, HY_FILTER_FFN), HY_POS_EMB ** -0.5)
    hy_ffn_b1 = nrm((L, HY_FILTER_FFN), 0.02)
    hy_sin_f1 = 1.0 + nrm((L, HY_FILTER_FFN), 0.05)
    hy_ffn_w2 = nrm((L, HY_FILTER_FFN, HY_FILTER_FFN), HY_FILTER_FFN ** -0.5)
    hy_ffn_b2 = nrm((L, HY_FILTER_FFN), 0.02)
    hy_sin_f2 = 1.0 + nrm((L, HY_FILTER_FFN), 0.05)
    hy_ffn_w3 = nrm((L, HY_FILTER_FFN, 2 * HY_ORDER * HY_WIDTH), HY_FILTER_FFN ** -0.5)
    hy_skip = nrm((L, HY_ORDER, HY_WIDTH), 1.0)
    lru_conv_w = nrm((L, LRU_CONV, LRU_WIDTH), LRU_CONV ** -0.5)
    lru_conv_b = nrm((L, LRU_WIDTH), 0.02)
    lru_wa = nrm((L, 2, LRU_BLOCKS, LRU_BLOCK_WIDTH, LRU_BLOCK_WIDTH), LRU_BLOCK_WIDTH ** -0.5)
    lru_ba = nrm((L, 2, LRU_WIDTH), 0.02)
    lru_wx = nrm((L, 2, LRU_BLOCKS, LRU_BLOCK_WIDTH, LRU_BLOCK_WIDTH), LRU_BLOCK_WIDTH ** -0.5)
    lru_bx = nrm((L, 2, LRU_WIDTH), 0.02)
    a_c = jax.random.uniform(next(ks), (L, 2, LRU_WIDTH), F32, minval=0.9, maxval=0.999)
    a0 = a_c ** (1.0 / LRU_C)
    lru_lambda = jnp.log(a0) - jnp.log1p(-a0)
    mla_q_norm_g = 1.0 + nrm((L, MLA_Q_LORA), 0.02)
    mla_w_uq = nrm((L, MLA_Q_LORA, MLA_HEADS * (MLA_NOPE_DIM + MLA_ROPE_DIM)), MLA_Q_LORA ** -0.5)
    mla_kv_norm_g = 1.0 + nrm((L, MLA_KV_LORA), 0.02)
    mla_w_ukv = nrm((L, MLA_KV_LORA, MLA_HEADS * (MLA_NOPE_DIM + MLA_V_DIM)), MLA_KV_LORA ** -0.5)
    group_norm_g = 1.0 + nrm((L, D_MIX), 0.02)
    w_out = nrm((L, D_MIX, D_MODEL), beta * D_MIX ** -0.5)
    ln_mix_g = 1.0 + nrm((L, D_MODEL), 0.02)
    ln_mix_b = nrm((L, D_MODEL), 0.02)
    router_w = nrm((L, D_MODEL, N_EXPERTS), D_MODEL ** -0.5)
    exp_w_gate = nrm((L, N_EXPERTS, D_MODEL, EXPERT_FF), D_MODEL ** -0.5)
    exp_w_up = nrm((L, N_EXPERTS, D_MODEL, EXPERT_FF), D_MODEL ** -0.5)
    exp_w_down = nrm((L, N_EXPERTS, EXPERT_FF, D_MODEL), beta * EXPERT_FF ** -0.5)
    ln_ffn_g = 1.0 + nrm((L, D_MODEL), 0.02)
    ln_ffn_b = nrm((L, D_MODEL), 0.02)
    return {"x": x, "ln_in_g": ln_in_g, "ln_in_b": ln_in_b, "w_in": w_in,
            "hy_conv_w": hy_conv_w, "hy_conv_b": hy_conv_b, "hy_ffn_w1": hy_ffn_w1, "hy_ffn_b1": hy_ffn_b1,
            "hy_sin_f1": hy_sin_f1, "hy_ffn_w2": hy_ffn_w2, "hy_ffn_b2": hy_ffn_b2, "hy_sin_f2": hy_sin_f2,
            "hy_ffn_w3": hy_ffn_w3, "hy_skip": hy_skip, "lru_conv_w": lru_conv_w, "lru_conv_b": lru_conv_b,
            "lru_wa": lru_wa, "lru_ba": lru_ba, "lru_wx": lru_wx, "lru_bx": lru_bx, "lru_lambda": lru_lambda,
            "mla_q_norm_g": mla_q_norm_g, "mla_w_uq": mla_w_uq, "mla_kv_norm_g": mla_kv_norm_g,
            "mla_w_ukv": mla_w_ukv, "group_norm_g": group_norm_g, "w_out": w_out,
            "ln_mix_g": ln_mix_g, "ln_mix_b": ln_mix_b, "router_w": router_w, "exp_w_gate": exp_w_gate,
            "exp_w_up": exp_w_up, "exp_w_down": exp_w_down, "ln_ffn_g": ln_ffn_g, "ln_ffn_b": ln_ffn_b}


def reference(x, ln_in_g, ln_in_b, w_in, hy_conv_w, hy_conv_b, hy_ffn_w1, hy_ffn_b1, hy_sin_f1,
              hy_ffn_w2, hy_ffn_b2, hy_sin_f2, hy_ffn_w3, hy_skip, lru_conv_w, lru_conv_b, lru_wa, lru_ba,
              lru_wx, lru_bx, lru_lambda, mla_q_norm_g, mla_w_uq, mla_kv_norm_g, mla_w_ukv, group_norm_g,
              w_out, ln_mix_g, ln_mix_b, router_w, exp_w_gate, exp_w_up, exp_w_down, ln_ffn_g, ln_ffn_b):
    alpha = (2.0 * DEPTH) ** 0.25
    cos, sin = _rope_tables(x.shape[1])
    x = _layer_norm(x, ln_in_g, ln_in_b)
    g0, g1 = HY_WIDTH, HY_WIDTH + LRU_WIDTH
    for l in range(DEPTH):
        proj = x @ w_in[l]
        p_hy, p_lr, p_lg, p_cq, p_ckv, p_kr = jnp.split(proj, IN_SPLITS, axis=-1)
        y_hy = _hyena_mixer(p_hy, hy_conv_w[l], hy_conv_b[l], hy_ffn_w1[l], hy_ffn_b1[l], hy_sin_f1[l],
                            hy_ffn_w2[l], hy_ffn_b2[l], hy_sin_f2[l], hy_ffn_w3[l], hy_skip[l])
        y_lr = _rglru_mixer(p_lr, p_lg, lru_conv_w[l], lru_conv_b[l], lru_wa[l], lru_ba[l],
                            lru_wx[l], lru_bx[l], lru_lambda[l])
        y_at = _mla_mixer(p_cq, p_ckv, p_kr, mla_q_norm_g[l], mla_w_uq[l], mla_kv_norm_g[l],
                          mla_w_ukv[l], cos, sin)
        gn = group_norm_g[l]
        y = jnp.concatenate([_rms_norm(y_hy, gn[:g0]), _rms_norm(y_lr, gn[g0:g1]),
                             _rms_norm(y_at, gn[g1:])], axis=-1)
        x = _layer_norm(alpha * x + y @ w_out[l], ln_mix_g[l], ln_mix_b[l])
        moe = _expert_choice_moe(x, router_w[l], exp_w_gate[l], exp_w_up[l], exp_w_down[l])
        x = _layer_norm(alpha * x + moe, ln_ffn_g[l], ln_ffn_b[l])
    return x
```

```python
import functools
import math

import jax
import jax.numpy as jnp
from jax import lax
from jax.experimental import pallas as pl
from jax.experimental.pallas import tpu as pltpu

F32 = jnp.float32
BF16 = jnp.bfloat16
HIGHEST = lax.Precision.HIGHEST

D_MODEL = 1024
HY_WIDTH = 256
LRU_WIDTH = 256
LRU_BLOCKS = 4
MLA_HEADS = 8
MLA_NOPE = 64
MLA_ROPE = 32
MLA_V = 64
MLA_Q_LORA = 768
MLA_KV_LORA = 256
ROPE_THETA = 10000.0
HY_POS_EMB = 33
HY_FILTER_FFN = 64
HY_FAST_DECAY = 0.3
HY_SLOW_DECAY = 1.5
HY_DECAY_TARGET = 1e-2
LRU_C = 8.0
N_EXPERTS = 16
EC_CAPACITY_FACTOR = 2
EPS = 1e-5
DEPTH = 4
ALPHA = (2.0 * DEPTH) ** 0.25

LANE = 128
MXU_N = 256
SUBLANES = 8
ROW_CHUNK = 256
HEAD_PAD = 128
PROJ_W = 2432
COL_HY, COL_CQ, COL_LR, COL_LG, COL_CKV, COL_KR = 0, 768, 1536, 1792, 2048, 2304
KR_LANE = 64
VMEM_CAP = 64 * 1024 * 1024


def _cparams(sem, vmem_mb):
    return pltpu.CompilerParams(dimension_semantics=sem, vmem_limit_bytes=vmem_mb * 1024 * 1024)


def _const_spec(shape):
    nd = len(shape)
    return pl.BlockSpec(shape, lambda *_: (0,) * nd, pipeline_mode=pl.Buffered(1))


def _layer_norm(x, g, b):
    xc = x - jnp.mean(x, axis=-1, keepdims=True)
    var = jnp.mean(xc * xc, axis=-1, keepdims=True)
    return xc * lax.rsqrt(var + EPS) * g + b


def _rms_norm(x, g):
    return x * lax.rsqrt(jnp.mean(x * x, axis=-1, keepdims=True) + EPS) * g


def _chunk_shift(load_row, x, r0, d, seq):
    rc = x.shape[0]
    if d == 0:
        return x
    t = lax.broadcasted_iota(jnp.int32, x.shape, 0)
    y = pltpu.roll(x, d % rc, axis=0)
    for k in range(abs(d)):
        dst = k if d > 0 else rc + d + k
        src = r0 - d + dst
        row = load_row(src) if 0 <= src < seq else jnp.zeros((1, x.shape[1]), x.dtype)
        y = jnp.where(t == dst, row, y)
    return y


def _sigmoid(x):
    return 1.0 / (1.0 + jnp.exp(-x))


def _gelu_tanh(x):
    c = math.sqrt(2.0 / math.pi)
    return 0.5 * x * (1.0 + jnp.tanh(c * (x + 0.044715 * (x * x * x))))


def _ln_kernel(x_ref, g_ref, b_ref, o_ref):
    o_ref[...] = _layer_norm(x_ref[...], g_ref[...], b_ref[...])


def _ln_call(x2d, g, b, tm):
    t, d = x2d.shape
    return pl.pallas_call(
        _ln_kernel,
        out_shape=jax.ShapeDtypeStruct((t, d), F32),
        grid=(t // tm,),
        in_specs=[pl.BlockSpec((tm, d), lambda i: (i, 0)), _const_spec((1, d)), _const_spec((1, d))],
        out_specs=pl.BlockSpec((tm, d), lambda i: (i, 0)),
        compiler_params=_cparams(("parallel",), 32),
        name="ln_in",
    )(x2d, g, b)


def _proj_kernel(x_ref, w_ref, o_ref):
    o_ref[...] = jnp.dot(x_ref[...].astype(BF16), w_ref[...], preferred_element_type=F32)


def _proj_call(x2d, w, tm):
    t, d = x2d.shape
    n = w.shape[1]
    return pl.pallas_call(
        _proj_kernel,
        out_shape=jax.ShapeDtypeStruct((t, n), F32),
        grid=(t // tm,),
        in_specs=[pl.BlockSpec((tm, d), lambda i: (i, 0)), _const_spec((d, n))],
        out_specs=pl.BlockSpec((tm, n), lambda i: (i, 0)),
        compiler_params=_cparams(("parallel",), 48),
        name="proj",
    )(x2d, w)


def _dft_table_kernel(c32_ref, s32_ref, cbf_ref, sbf_ref, tc_scr, ts_scr, *, seq, rb):
    j = pl.program_id(0)
    mask = 2 * seq - 1
    step = math.pi / seq

    @pl.when(j == 0)
    def _():
        fi = lax.broadcasted_iota(jnp.int32, (rb, seq), 0)
        t = lax.broadcasted_iota(jnp.int32, (rb, seq), 1)
        ang = ((fi * t) & mask).astype(F32) * step
        tc_scr[...] = jnp.cos(ang)
        ts_scr[...] = jnp.sin(ang)

    t = lax.broadcasted_iota(jnp.int32, (SUBLANES, seq), 1)
    ang = (((j * rb) * t) & mask).astype(F32) * step
    ca = jnp.cos(ang)[0:1, :]
    sa = jnp.sin(ang)[0:1, :]
    c = ca * tc_scr[...] - sa * ts_scr[...]
    s = sa * tc_scr[...] + ca * ts_scr[...]
    c32_ref[...] = c
    s32_ref[...] = s
    cbf_ref[...] = c.astype(BF16)
    sbf_ref[...] = s.astype(BF16)


def _dft_table_call(seq):
    assert seq & (seq - 1) == 0, "sequence length must be a power of two"
    rb = min(256, seq)
    blk = pl.BlockSpec((rb, seq), lambda j: (j, 0))
    return pl.pallas_call(
        functools.partial(_dft_table_kernel, seq=seq, rb=rb),
        out_shape=(jax.ShapeDtypeStruct((seq, seq), F32), jax.ShapeDtypeStruct((seq, seq), F32),
                   jax.ShapeDtypeStruct((seq, seq), BF16), jax.ShapeDtypeStruct((seq, seq), BF16)),
        grid=(seq // rb,),
        in_specs=[],
        out_specs=(blk, blk, blk, blk),
        scratch_shapes=[pltpu.VMEM((rb, seq), F32), pltpu.VMEM((rb, seq), F32)],
        compiler_params=_cparams(("arbitrary",), 40),
        name="dft_tables",
    )()


def _hyena_filter_kernel(z_ref, w1_ref, b1_ref, f1_ref, w2_ref, b2_ref, f2_ref, w3_ref, dl_ref,
                         c_ref, s_ref, ar_ref, ai_ref, any_ref, hs_scr, hd_scr, *, seq, fb):
    j = pl.program_id(1)
    n_fft = 2 * seq

    @pl.when(j == 0)
    def _():
        z = z_ref[...]
        h = jnp.sin(f1_ref[0] * (jnp.dot(z, w1_ref[0], precision=HIGHEST, preferred_element_type=F32)
                                 + b1_ref[0]))
        h = jnp.sin(f2_ref[0] * (jnp.dot(h, w2_ref[0], precision=HIGHEST, preferred_element_type=F32)
                                 + b2_ref[0]))
        t_idx = lax.broadcasted_iota(jnp.int32, (seq, HY_WIDTH), 0)
        t = t_idx.astype(F32) * (1.0 / (seq - 1))
        window = jnp.exp(-t * dl_ref[...])
        sign = jnp.where(t_idx % 2 == 0, 1.0, -1.0)

        def filt(g):
            w3g = w3_ref[0, :, g * HY_WIDTH:(g + 1) * HY_WIDTH]
            return jnp.dot(h, w3g, precision=HIGHEST, preferred_element_type=F32) * window

        for o in range(2):
            hf = filt(o)
            hb = filt(2 + o)
            hb0 = jnp.where(t_idx == 0, 0.0, hb)
            hs = hf + hb0
            hs_scr[:, o * HY_WIDTH:(o + 1) * HY_WIDTH] = hs
            hd_scr[:, o * HY_WIDTH:(o + 1) * HY_WIDTH] = hb0 - hf
            any_ref[0, o] = jnp.sum(hs * sign, axis=0, keepdims=True) * (1.0 / n_fft)

    kr = jnp.dot(c_ref[...], hs_scr[...], precision=HIGHEST, preferred_element_type=F32)
    ki = jnp.dot(s_ref[...], hd_scr[...], precision=HIGHEST, preferred_element_type=F32)
    f_idx = lax.broadcasted_iota(jnp.int32, (fb, 2 * HY_WIDTH), 0) + j * fb
    wf = jnp.where(f_idx == 0, 1.0 / n_fft, 2.0 / n_fft)
    kr = kr * wf
    ki = ki * wf
    for o in range(2):
        ar_ref[0, o] = kr[:, o * HY_WIDTH:(o + 1) * HY_WIDTH]
        ai_ref[0, o] = ki[:, o * HY_WIDTH:(o + 1) * HY_WIDTH]


def _hyena_filter_call(ztab, w1, b1, f1, w2, b2, f2, w3, deltas, c32, s32, seq):
    depth = w1.shape[0]
    fb = min(256, seq)
    nf = seq // fb
    kern = functools.partial(_hyena_filter_kernel, seq=seq, fb=fb)
    lay = lambda l, j: (l, 0, 0)
    return pl.pallas_call(
        kern,
        out_shape=(jax.ShapeDtypeStruct((depth, 2, seq, HY_WIDTH), F32),
                   jax.ShapeDtypeStruct((depth, 2, seq, HY_WIDTH), F32),
                   jax.ShapeDtypeStruct((depth, 2, 1, HY_WIDTH), F32)),
        grid=(depth, nf),
        in_specs=[
            _const_spec(ztab.shape),
            pl.BlockSpec((1,) + w1.shape[1:], lay), pl.BlockSpec((1,) + b1.shape[1:], lay),
            pl.BlockSpec((1,) + f1.shape[1:], lay),
            pl.BlockSpec((1,) + w2.shape[1:], lay), pl.BlockSpec((1,) + b2.shape[1:], lay),
            pl.BlockSpec((1,) + f2.shape[1:], lay),
            pl.BlockSpec((1,) + w3.shape[1:], lay),
            _const_spec(deltas.shape),
            pl.BlockSpec((fb, seq), lambda l, j: (j, 0)),
            pl.BlockSpec((fb, seq), lambda l, j: (j, 0)),
        ],
        out_specs=(pl.BlockSpec((1, 2, fb, HY_WIDTH), lambda l, j: (l, 0, j, 0)),
                   pl.BlockSpec((1, 2, fb, HY_WIDTH), lambda l, j: (l, 0, j, 0)),
                   pl.BlockSpec((1, 2, 1, HY_WIDTH), lambda l, j: (l, 0, 0, 0))),
        scratch_shapes=[pltpu.VMEM((seq, 2 * HY_WIDTH), F32), pltpu.VMEM((seq, 2 * HY_WIDTH), F32)],
        compiler_params=_cparams(("arbitrary", "arbitrary"), 48),
        name="hyena_filter",
    )(ztab, w1, b1, f1, w2, b2, f2, w3, deltas, c32, s32)


def _hyena_kernel(p_ref, cw_ref, cb_ref, c_ref, s_ref, ar_ref, ai_ref, any_ref, sk_ref, o_ref,
                  g_scr, z_scr, zb_scr, acc_scr, *, seq, fb):
    w = HY_WIDTH
    rc = min(ROW_CHUNK, seq)
    load_row = lambda r: p_ref[0, r:r + 1, :]
    for r0 in range(0, seq, rc):
        x = p_ref[0, r0:r0 + rc, :]
        u = (cw_ref[0:1, :] * _chunk_shift(load_row, x, r0, 1, seq) + cw_ref[1:2, :] * x
             + cw_ref[2:3, :] * _chunk_shift(load_row, x, r0, -1, seq) + cb_ref[...])
        g_scr[r0:r0 + rc, :] = u[:, 0:2 * w]
        z_scr[r0:r0 + rc, :] = u[:, 2 * w:3 * w]
    t_idx = lax.broadcasted_iota(jnp.int32, (seq, w), 0)
    sign = jnp.where(t_idx % 2 == 0, 1.0, -1.0)
    for o in range(2):
        z = z_scr[...]
        zb_scr[...] = z.astype(BF16)
        acc_scr[...] = sign * (jnp.sum(z * sign, axis=0, keepdims=True) * any_ref[0, o])
        for j in range(seq // fb):
            rows = slice(j * fb, (j + 1) * fb)
            ur = jnp.dot(c_ref[rows, :], zb_scr[...], preferred_element_type=F32)
            us = jnp.dot(s_ref[rows, :], zb_scr[...], preferred_element_type=F32)
            ar = ar_ref[0, o, rows, :]
            ai = ai_ref[0, o, rows, :]
            pp = (ur * ar + us * ai).astype(BF16)
            qq = (us * ar - ur * ai).astype(BF16)
            acc_scr[...] += (jnp.dot(c_ref[:, rows], pp, preferred_element_type=F32)
                             + jnp.dot(s_ref[:, rows], qq, preferred_element_type=F32))
        z_new = g_scr[:, o * w:(o + 1) * w] * (acc_scr[...] + sk_ref[0, o] * z_scr[...])
        if o == 0:
            z_scr[...] = z_new
        else:
            o_ref[0] = z_new


def _hyena_call(proj, cw, cb, cbf, sbf, ar, ai, any_, skip, layer):
    bsz, seq, _ = proj.shape
    fb = seq
    kern = functools.partial(_hyena_kernel, seq=seq, fb=fb)
    lsel = lambda b: (layer, 0, 0, 0)
    return pl.pallas_call(
        kern,
        out_shape=jax.ShapeDtypeStruct((bsz, seq, HY_WIDTH), F32),
        grid=(bsz,),
        in_specs=[
            pl.BlockSpec((1, seq, 3 * HY_WIDTH), lambda b: (b, 0, COL_HY // (3 * HY_WIDTH)),
                         pipeline_mode=pl.Buffered(1)),
            _const_spec(cw.shape), _const_spec(cb.shape),
            _const_spec(cbf.shape), _const_spec(sbf.shape),
            pl.BlockSpec((1, 2, seq, HY_WIDTH), lsel, pipeline_mode=pl.Buffered(1)),
            pl.BlockSpec((1, 2, seq, HY_WIDTH), lsel, pipeline_mode=pl.Buffered(1)),
            pl.BlockSpec((1, 2, 1, HY_WIDTH), lsel, pipeline_mode=pl.Buffered(1)),
            pl.BlockSpec((1, 2, 1, HY_WIDTH), lsel, pipeline_mode=pl.Buffered(1)),
        ],
        out_specs=pl.BlockSpec((1, seq, HY_WIDTH), lambda b: (b, 0, 0)),
        scratch_shapes=[pltpu.VMEM((seq, 2 * HY_WIDTH), F32), pltpu.VMEM((seq, HY_WIDTH), F32),
                        pltpu.VMEM((seq, HY_WIDTH), BF16), pltpu.VMEM((seq, HY_WIDTH), F32)],
        compiler_params=_cparams(("parallel",), 60),
        name="hyena",
    )(proj, cw, cb, cbf, sbf, ar, ai, any_, skip)


def _lru_kernel(xr_ref, xg_ref, cw_ref, cb_ref, wg_ref, bg_ref, lam_ref, o_ref, a_scr, u_scr, *, seq):
    w = LRU_WIDTH
    rc = min(ROW_CHUNK, seq)
    lam = lam_ref[...]
    y = jnp.exp(-jnp.abs(lam))
    w1 = 1.0 + y
    log1p_y = jnp.where(w1 == 1.0, y, jnp.log(w1) * (y / jnp.where(w1 == 1.0, 1.0, w1 - 1.0)))
    softplus_neg = jnp.maximum(-lam, 0.0) + log1p_y
    load_row = lambda r: xr_ref[0, r:r + 1, :]
    t_in = lax.broadcasted_iota(jnp.int32, (rc, w), 0) % SUBLANES
    for r0 in range(0, seq, rc):
        x = xr_ref[0, r0:r0 + rc, :]
        xc = (cw_ref[0:1, :] * _chunk_shift(load_row, x, r0, 2, seq)
              + cw_ref[1:2, :] * _chunk_shift(load_row, x, r0, 1, seq)
              + cw_ref[2:3, :] * x + cw_ref[3:4, :] * _chunk_shift(load_row, x, r0, -1, seq) + cb_ref[...])
        xcb = xc.astype(BF16)
        for d in range(2):
            ca, cx = slice(d * w, (d + 1) * w), slice((2 + d) * w, (3 + d) * w)
            gate_a = _sigmoid(jnp.dot(xcb, wg_ref[:, ca], preferred_element_type=F32) + bg_ref[:, ca])
            gate_x = _sigmoid(jnp.dot(xcb, wg_ref[:, cx], preferred_element_type=F32) + bg_ref[:, cx])
            a = jnp.exp(-LRU_C * gate_a * softplus_neg[:, ca])
            u = jnp.sqrt(1.0 - a * a) * gate_x * xc
            s = 1
            while s < SUBLANES:
                keep = (t_in >= s) if d == 0 else (t_in < SUBLANES - s)
                sh = s if d == 0 else rc - s
                u = u + a * jnp.where(keep, pltpu.roll(u, sh, axis=0), 0.0)
                a = a * jnp.where(keep, pltpu.roll(a, sh, axis=0), 1.0)
                s *= 2
            a_scr[d, r0:r0 + rc, :] = a
            u_scr[d, r0:r0 + rc, :] = u

    ng = seq // SUBLANES

    def carry_step(i, carry):
        cf, cb = carry
        rf = pl.multiple_of(i * SUBLANES, SUBLANES)
        rb = pl.multiple_of((ng - 1 - i) * SUBLANES, SUBLANES)
        hf = u_scr[0, pl.ds(rf, SUBLANES), :] + a_scr[0, pl.ds(rf, SUBLANES), :] * cf
        hb = u_scr[1, pl.ds(rb, SUBLANES), :] + a_scr[1, pl.ds(rb, SUBLANES), :] * cb
        u_scr[0, pl.ds(rf, SUBLANES), :] = hf
        u_scr[1, pl.ds(rb, SUBLANES), :] = hb
        return hf[SUBLANES - 1:SUBLANES, :], hb[0:1, :]

    zero = jnp.zeros((1, w), F32)
    lax.fori_loop(0, ng, carry_step, (zero, zero), unroll=4)
    for r0 in range(0, seq, rc):
        rows = slice(r0, r0 + rc)
        o_ref[0, rows, :] = (u_scr[0, rows, :] + u_scr[1, rows, :]) * _gelu_tanh(xg_ref[0, rows, :])


def _lru_call(proj, cw, cb, wg, bg, lam):
    bsz, seq, _ = proj.shape
    w = LRU_WIDTH
    return pl.pallas_call(
        functools.partial(_lru_kernel, seq=seq),
        scratch_shapes=[pltpu.VMEM((2, seq, w), F32), pltpu.VMEM((2, seq, w), F32)],
        out_shape=jax.ShapeDtypeStruct((bsz, seq, w), F32),
        grid=(bsz,),
        in_specs=[
            pl.BlockSpec((1, seq, w), lambda b: (b, 0, COL_LR // w)),
            pl.BlockSpec((1, seq, w), lambda b: (b, 0, COL_LG // w)),
            _const_spec(cw.shape), _const_spec(cb.shape), _const_spec(wg.shape),
            _const_spec(bg.shape), _const_spec(lam.shape),
        ],
        out_specs=pl.BlockSpec((1, seq, w), lambda b: (b, 0, 0)),
        compiler_params=_cparams(("parallel",), 48),
        name="rglru",
    )(proj, proj, cw, cb, wg, bg, lam)


def _rope128(x, cos, sin):
    lane = lax.broadcasted_iota(jnp.int32, x.shape, 1)
    half = MLA_ROPE // 2
    partner = jnp.where(lane < KR_LANE + half, pltpu.roll(x, LANE - half, axis=1),
                        pltpu.roll(x, half, axis=1))
    return x * cos + partner * sin


def _mla_prep_kernel(cq_ref, ckv_ref, kr_ref, qg_ref, wq_ref, kg_ref, wk_ref, wv_ref, one_ref, cos_ref,
                     sin_ref, q_ref, k_ref, v_ref, *, qscale):
    cos = cos_ref[...]
    sin = sin_ref[...]
    qn = _rms_norm(cq_ref[0], qg_ref[...]).astype(BF16)
    q = jnp.dot(qn, wq_ref[...], preferred_element_type=F32)
    kvn = _rms_norm(ckv_ref[0], kg_ref[...]).astype(BF16)
    kn = jnp.dot(kvn, wk_ref[...], preferred_element_type=F32)
    v_ref[0] = (jnp.dot(kvn, wv_ref[...], preferred_element_type=F32) + one_ref[...]).astype(BF16)
    kr = _rope128(kr_ref[0], cos, sin)
    for h in range(MLA_HEADS):
        cols = slice(h * HEAD_PAD, (h + 1) * HEAD_PAD)
        q_ref[0, :, cols] = (_rope128(q[:, cols], cos, sin) * qscale).astype(BF16)
        k_ref[0, :, cols] = (kn[:, cols] + kr).astype(BF16)


def _mla_prep_call(proj, qg, wq, kg, wk, wv, onescol, cos_t, sin_t):
    bsz, seq, _ = proj.shape
    hp = MLA_HEADS * HEAD_PAD
    hv = hp
    tr = min(512, seq)
    qscale = float((MLA_NOPE + MLA_ROPE) ** -0.5 * math.log2(math.e))
    return pl.pallas_call(
        functools.partial(_mla_prep_kernel, qscale=qscale),
        out_shape=(jax.ShapeDtypeStruct((bsz, seq, hp), BF16),
                   jax.ShapeDtypeStruct((bsz, seq, hp), BF16),
                   jax.ShapeDtypeStruct((bsz, seq, hv), BF16)),
        grid=(bsz, seq // tr),
        in_specs=[
            pl.BlockSpec((1, tr, MLA_Q_LORA), lambda b, i: (b, i, COL_CQ // MLA_Q_LORA)),
            pl.BlockSpec((1, tr, MLA_KV_LORA), lambda b, i: (b, i, COL_CKV // MLA_KV_LORA)),
            pl.BlockSpec((1, tr, LANE), lambda b, i: (b, i, COL_KR // LANE)),
            _const_spec(qg.shape), _const_spec(wq.shape), _const_spec(kg.shape),
            _const_spec(wk.shape), _const_spec(wv.shape), _const_spec(onescol.shape),
            pl.BlockSpec((tr, LANE), lambda b, i: (i, 0)), pl.BlockSpec((tr, LANE), lambda b, i: (i, 0)),
        ],
        out_specs=(pl.BlockSpec((1, tr, hp), lambda b, i: (b, i, 0)),
                   pl.BlockSpec((1, tr, hp), lambda b, i: (b, i, 0)),
                   pl.BlockSpec((1, tr, hv), lambda b, i: (b, i, 0))),
        compiler_params=_cparams(("parallel", "parallel"), 48),
        name="mla_prep",
    )(proj, proj, proj, qg, wq, kg, wk, wv, onescol, cos_t, sin_t)


def _attn_kernel(q_ref, k_ref, v_ref, o_ref):
    tq = q_ref.shape[1]
    outs = []
    for h in range(2):
        cols = slice(h * HEAD_PAD, (h + 1) * HEAD_PAD)
        s = lax.dot_general(q_ref[0, :, cols], k_ref[0, :, cols], (((1,), (1,)), ((), ())),
                            preferred_element_type=F32)
        p = jnp.exp2(s - jnp.max(s, axis=-1, keepdims=True)).astype(BF16)
        r = jnp.dot(p, v_ref[0, :, cols], preferred_element_type=F32)
        outs.append(r / r[:, MLA_V:MLA_V + 1])
    lane = lax.broadcasted_iota(jnp.int32, (tq, HEAD_PAD), 1)
    o_ref[0] = jnp.where(lane < MLA_V, outs[0], pltpu.roll(outs[1], MLA_V, axis=1))


def _attn_call(q, k, v):
    bsz, seq, hp = q.shape
    tq = min(512, seq)
    pw = 2 * HEAD_PAD
    return pl.pallas_call(
        _attn_kernel,
        out_shape=jax.ShapeDtypeStruct((bsz, seq, MLA_HEADS * MLA_V), F32),
        grid=(bsz, MLA_HEADS // 2, seq // tq),
        in_specs=[pl.BlockSpec((1, tq, pw), lambda b, j, i: (b, i, j)),
                  pl.BlockSpec((1, seq, pw), lambda b, j, i: (b, 0, j)),
                  pl.BlockSpec((1, seq, pw), lambda b, j, i: (b, 0, j))],
        out_specs=pl.BlockSpec((1, tq, HEAD_PAD), lambda b, j, i: (b, i, j)),
        compiler_params=_cparams(("parallel", "parallel", "parallel"), 48),
        name="mla_attn",
    )(q, k, v)


def _postmix_kernel(yh_ref, yl_ref, ya_ref, x_ref, gn_ref, wo_ref, g_ref, b_ref, o_ref, ob_ref):
    g0, g1 = HY_WIDTH, HY_WIDTH + LRU_WIDTH
    y = jnp.concatenate([_rms_norm(yh_ref[...], gn_ref[:, 0:g0]),
                         _rms_norm(yl_ref[...], gn_ref[:, g0:g1]),
                         _rms_norm(ya_ref[...], gn_ref[:, g1:])], axis=-1).astype(BF16)
    mix = jnp.dot(y, wo_ref[...], preferred_element_type=F32)
    x1 = _layer_norm(ALPHA * x_ref[...] + mix, g_ref[...], b_ref[...])
    o_ref[...] = x1
    ob_ref[...] = x1.astype(BF16)


def _postmix_call(yh, yl, ya, x2d, gn, wo, g, b, tm):
    t, d = x2d.shape
    row = lambda w: pl.BlockSpec((tm, w), lambda i: (i, 0))
    return pl.pallas_call(
        _postmix_kernel,
        out_shape=(jax.ShapeDtypeStruct((t, d), F32), jax.ShapeDtypeStruct((t, d), BF16)),
        grid=(t // tm,),
        in_specs=[row(HY_WIDTH), row(LRU_WIDTH), row(MLA_HEADS * MLA_V), row(d),
                  _const_spec(gn.shape), _const_spec(wo.shape), _const_spec(g.shape), _const_spec(b.shape)],
        out_specs=(row(d), row(d)),
        compiler_params=_cparams(("parallel",), 48),
        name="postmix",
    )(yh, yl, ya, x2d, gn, wo, g, b)


def _cumsum_lanes(x):
    n = x.shape[1]
    lane = lax.broadcasted_iota(jnp.int32, x.shape, 1)
    d = 1
    while d < n:
        x = x + jnp.where(lane >= d, pltpu.roll(x, d, axis=1), 0.0)
        d *= 2
    return x


def _select_kernel(x_ref, rw_ref, pos_ref, post_ref, gatet_ref, *, cap):
    x = x_ref[0]
    logits = jnp.dot(x, rw_ref[...], precision=HIGHEST, preferred_element_type=F32)
    lane = lax.broadcasted_iota(jnp.int32, logits.shape, 1)
    valid = lane < N_EXPERTS
    logits = jnp.where(valid, logits, -1e30)
    m = jnp.max(logits, axis=-1, keepdims=True)
    ex = jnp.where(valid, jnp.exp(logits - m), 0.0)
    aff = ex / jnp.sum(ex, axis=-1, keepdims=True)
    seq = aff.shape[0]
    aff_e = aff.T[0:N_EXPERTS, :]

    def body(_, carry):
        lo, hi = carry
        mid = lo + ((hi - lo + 1) >> 1)
        cnt = jnp.sum(jnp.where(aff_e >= pltpu.bitcast(mid, F32), 1.0, 0.0), axis=1, keepdims=True)
        ok = cnt >= float(cap)
        return jnp.where(ok, mid, lo), jnp.where(ok, hi, mid - 1)

    lo0 = jnp.zeros((N_EXPERTS, 1), jnp.int32)
    hi0 = jnp.full((N_EXPERTS, 1), 0x7F7FFFFF, jnp.int32)
    thr_bits, _ = lax.fori_loop(0, 31, body, (lo0, hi0))
    thr = pltpu.bitcast(thr_bits, F32)
    gt = aff_e > thr
    eq = aff_e == thr
    need = float(cap) - jnp.sum(jnp.where(gt, 1.0, 0.0), axis=1, keepdims=True)
    eq_rank = _cumsum_lanes(jnp.where(eq, 1.0, 0.0))
    sel = gt | (eq & (eq_rank <= need))
    pos = jnp.where(sel, _cumsum_lanes(jnp.where(sel, 1.0, 0.0)) - 1.0, -1.0)
    pos_ref[0] = pos
    fill = jnp.full((LANE - N_EXPERTS, seq), -1.0, F32)
    post_ref[0] = jnp.concatenate([pos, fill], axis=0).T
    gatet_ref[0] = jnp.concatenate([jnp.where(sel, aff_e, 0.0), fill], axis=0).T


def _select_call(x1, rw, cap):
    bsz, seq, d = x1.shape
    kern = functools.partial(_select_kernel, cap=cap)
    return pl.pallas_call(
        kern,
        out_shape=(jax.ShapeDtypeStruct((bsz, N_EXPERTS, seq), F32),
                   jax.ShapeDtypeStruct((bsz, seq, LANE), F32),
                   jax.ShapeDtypeStruct((bsz, seq, LANE), F32)),
        grid=(bsz,),
        in_specs=[pl.BlockSpec((1, seq, d), lambda b: (b, 0, 0)), _const_spec(rw.shape)],
        out_specs=(pl.BlockSpec((1, N_EXPERTS, seq), lambda b: (b, 0, 0)),
                   pl.BlockSpec((1, seq, LANE), lambda b: (b, 0, 0)),
                   pl.BlockSpec((1, seq, LANE), lambda b: (b, 0, 0))),
        compiler_params=_cparams(("parallel",), 48),
        name="moe_select",
    )(x1, rw)


def _gather_kernel(pos_ref, x_ref, o_ref, p_scr, *, cap):
    seq, d = x_ref.shape[1], x_ref.shape[2]
    slot = lax.broadcasted_iota(jnp.int32, (cap, seq), 0).astype(F32)
    for e in range(N_EXPERTS):
        p_scr[e * cap:(e + 1) * cap, :] = jnp.where(pos_ref[0, e:e + 1, :] == slot, 1.0, 0.0).astype(BF16)
    cw = min(MXU_N, d)
    for c in range(0, d, cw):
        o_ref[0, :, c:c + cw] = jnp.dot(p_scr[...], x_ref[0, :, c:c + cw],
                                        preferred_element_type=F32).astype(BF16)


def _gather_call(pos, x1b, cap):
    bsz, seq, d = x1b.shape
    return pl.pallas_call(
        functools.partial(_gather_kernel, cap=cap),
        out_shape=jax.ShapeDtypeStruct((bsz, N_EXPERTS * cap, d), BF16),
        grid=(bsz,),
        in_specs=[pl.BlockSpec((1, N_EXPERTS, seq), lambda b: (b, 0, 0)),
                  pl.BlockSpec((1, seq, d), lambda b: (b, 0, 0))],
        out_specs=pl.BlockSpec((1, N_EXPERTS * cap, d), lambda b: (b, 0, 0)),
        scratch_shapes=[pltpu.VMEM((N_EXPERTS * cap, seq), BF16)],
        compiler_params=_cparams(("parallel",), 56),
        name="moe_gather",
    )(pos, x1b)


def _ffn_kernel(xe_ref, wg_ref, wu_ref, wd_ref, o_ref, acc_scr, *, rows_per_dot):
    f = pl.program_id(1)
    bsz, cap, d = xe_ref.shape
    wg = wg_ref[0, 0].astype(BF16)
    wu = wu_ref[0, 0].astype(BF16)
    wd = wd_ref[0, 0].astype(BF16)
    nb = rows_per_dot // cap
    for b0 in range(0, bsz, nb):
        xe = xe_ref[b0:b0 + nb].reshape(nb * cap, d)
        hg = jnp.dot(xe, wg, preferred_element_type=F32)
        hu = jnp.dot(xe, wu, preferred_element_type=F32)
        hid = (hg * _sigmoid(hg) * hu).astype(BF16)
        part = jnp.dot(hid, wd, preferred_element_type=F32)
        rows = slice(b0 * cap, (b0 + nb) * cap)

        @pl.when(f == 0)
        def _():
            acc_scr[rows, :] = part

        @pl.when(f > 0)
        def _():
            acc_scr[rows, :] += part

    @pl.when(f == pl.num_programs(1) - 1)
    def _():
        o_ref[...] = acc_scr[...].reshape(bsz, cap, d).astype(BF16)


def _ffn_call(xe, wg, wu, wd, layer, cap):
    bsz, _, d = xe.shape
    _, ne, _, ff = wg.shape
    fcw = min(512, ff)
    rows_per_dot = cap * max(1, min(bsz, 1024 // cap))
    return pl.pallas_call(
        functools.partial(_ffn_kernel, rows_per_dot=rows_per_dot),
        out_shape=jax.ShapeDtypeStruct((bsz, ne * cap, d), BF16),
        grid=(ne, ff // fcw),
        in_specs=[pl.BlockSpec((bsz, cap, d), lambda e, f: (0, e, 0)),
                  pl.BlockSpec((1, 1, d, fcw), lambda e, f: (layer, e, 0, f)),
                  pl.BlockSpec((1, 1, d, fcw), lambda e, f: (layer, e, 0, f)),
                  pl.BlockSpec((1, 1, fcw, d), lambda e, f: (layer, e, f, 0))],
        out_specs=pl.BlockSpec((bsz, cap, d), lambda e, f: (0, e, 0)),
        scratch_shapes=[pltpu.VMEM((bsz * cap, d), F32)],
        compiler_params=_cparams(("parallel", "arbitrary"), 56),
        name="moe_ffn",
    )(xe, wg, wu, wd)


def _combine_kernel(post_ref, gatet_ref, ye_ref, x_ref, g_ref, b_ref, o_ref, *, cap):
    tr = x_ref.shape[1]
    slot = lax.broadcasted_iota(jnp.int32, (tr, cap), 1).astype(F32)
    post = post_ref[0]
    gatet = gatet_ref[0]
    acc = ALPHA * x_ref[0]
    for e in range(N_EXPERTS):
        scat = jnp.where(post[:, e:e + 1] == slot, gatet[:, e:e + 1], 0.0).astype(BF16)
        acc = acc + jnp.dot(scat, ye_ref[0, e * cap:(e + 1) * cap, :], preferred_element_type=F32)
    o_ref[0] = _layer_norm(acc, g_ref[...], b_ref[...])


def _combine_call(post, gatet, ye, x1, g, b, cap):
    bsz, seq, d = x1.shape
    tr = min(512, seq)
    kern = functools.partial(_combine_kernel, cap=cap)
    return pl.pallas_call(
        kern,
        out_shape=jax.ShapeDtypeStruct((bsz, seq, d), F32),
        grid=(bsz, seq // tr),
        in_specs=[pl.BlockSpec((1, tr, LANE), lambda bi, i: (bi, i, 0)),
                  pl.BlockSpec((1, tr, LANE), lambda bi, i: (bi, i, 0)),
                  pl.BlockSpec((1, N_EXPERTS * cap, d), lambda bi, i: (bi, 0, 0)),
                  pl.BlockSpec((1, tr, d), lambda bi, i: (bi, i, 0)),
                  _const_spec(g.shape), _const_spec(b.shape)],
        out_specs=pl.BlockSpec((1, tr, d), lambda bi, i: (bi, i, 0)),
        compiler_params=_cparams(("parallel", "parallel"), 48),
        name="moe_combine",
    )(post, gatet, ye, x1, g, b)


def _hyena_pos_table(seq):
    t = jnp.linspace(0.0, 1.0, seq, dtype=F32)[:, None]
    bands = (HY_POS_EMB - 1) // 2
    t_idx = jnp.arange(seq, dtype=F32)[:, None]
    freqs = jnp.linspace(1e-4, bands - 1, bands, dtype=F32)[None, :]
    ang = 2.0 * math.pi * t_idx * freqs / seq
    z = jnp.concatenate([t, jnp.cos(ang), -jnp.sin(ang)], axis=-1)
    return jnp.pad(z, ((0, 0), (0, LANE - HY_POS_EMB)))


def _rope_lane_tables(seq):
    half = MLA_ROPE // 2
    inv = ROPE_THETA ** (-jnp.arange(0, MLA_ROPE, 2, dtype=F32) / MLA_ROPE)
    ang = jnp.arange(seq, dtype=F32)[:, None] * inv[None, :]
    cos, sin = jnp.cos(ang), jnp.sin(ang)
    ones = jnp.ones((seq, KR_LANE), F32)
    zeros = jnp.zeros((seq, KR_LANE), F32)
    tail = LANE - KR_LANE - MLA_ROPE
    cos_t = jnp.concatenate([ones, cos, cos, jnp.ones((seq, tail), F32)], axis=-1)
    sin_t = jnp.concatenate([zeros, -sin, sin, jnp.zeros((seq, tail), F32)], axis=-1)
    return cos_t, sin_t


def _relayout_w_in(w_in):
    d = w_in.shape[1]
    hy, lr, lg, cq, ckv, kr = jnp.split(w_in, [768, 1024, 1280, 2048, 2304], axis=-1)
    z = lambda n: jnp.zeros(w_in.shape[:2] + (n,), w_in.dtype)
    out = jnp.concatenate([hy, cq, lr, lg, ckv, z(KR_LANE), kr, z(LANE - KR_LANE - MLA_ROPE)], axis=-1)
    assert out.shape[-1] == PROJ_W and d == D_MODEL
    return out.astype(BF16)


def _relayout_mla(w_uq, w_ukv):
    depth = w_uq.shape[0]
    qd = MLA_NOPE + MLA_ROPE
    wq = w_uq.reshape(depth, MLA_Q_LORA, MLA_HEADS, qd)
    wq = jnp.pad(wq, ((0, 0), (0, 0), (0, 0), (0, HEAD_PAD - qd)))
    wq = wq.reshape(depth, MLA_Q_LORA, MLA_HEADS * HEAD_PAD).astype(BF16)
    wkv = w_ukv.reshape(depth, MLA_KV_LORA, MLA_HEADS, MLA_NOPE + MLA_V)
    wk = jnp.pad(wkv[..., :MLA_NOPE], ((0, 0), (0, 0), (0, 0), (0, HEAD_PAD - MLA_NOPE)))
    wk = wk.reshape(depth, MLA_KV_LORA, MLA_HEADS * HEAD_PAD).astype(BF16)
    wv = jnp.pad(wkv[..., MLA_NOPE:], ((0, 0), (0, 0), (0, 0), (0, HEAD_PAD - MLA_V)))
    wv = wv.reshape(depth, MLA_KV_LORA, MLA_HEADS * HEAD_PAD).astype(BF16)
    onescol = jnp.tile(jnp.arange(HEAD_PAD) == MLA_V, MLA_HEADS).astype(F32)[None, :]
    return wq, wk, wv, onescol


def _relayout_lru_gates(wa, wx, ba, bx):
    depth = wa.shape[0]
    bw = LRU_WIDTH // LRU_BLOCKS
    eye = jnp.eye(LRU_BLOCKS, dtype=wa.dtype)

    def bd(w):
        full = jnp.einsum("ldnjk,nm->ldnjmk", w, eye)
        return full.reshape(depth, 2, LRU_WIDTH, LRU_WIDTH)

    a, x = bd(wa), bd(wx)
    wg = jnp.concatenate([a[:, 0], a[:, 1], x[:, 0], x[:, 1]], axis=-1).astype(BF16)
    bg = jnp.concatenate([ba[:, 0], ba[:, 1], bx[:, 0], bx[:, 1]], axis=-1)[:, None, :]
    del bw
    return wg, bg


def kernel(x, ln_in_g, ln_in_b, w_in, hy_conv_w, hy_conv_b, hy_ffn_w1, hy_ffn_b1, hy_sin_f1,
           hy_ffn_w2, hy_ffn_b2, hy_sin_f2, hy_ffn_w3, hy_skip, lru_conv_w, lru_conv_b, lru_wa, lru_ba,
           lru_wx, lru_bx, lru_lambda, mla_q_norm_g, mla_w_uq, mla_kv_norm_g, mla_w_ukv, group_norm_g,
           w_out, ln_mix_g, ln_mix_b, router_w, exp_w_gate, exp_w_up, exp_w_down, ln_ffn_g, ln_ffn_b):
    bsz, seq, d = x.shape
    depth = w_in.shape[0]
    t = bsz * seq
    tm = min(512, t)
    cap = max(1, EC_CAPACITY_FACTOR * seq // N_EXPERTS)
    row = lambda v: v.reshape(1, -1)

    c32, s32, cbf, sbf = _dft_table_call(seq)
    ztab = _hyena_pos_table(seq)
    max_decay = math.log(HY_DECAY_TARGET) / HY_FAST_DECAY
    min_decay = math.log(HY_DECAY_TARGET) / HY_SLOW_DECAY
    deltas = jnp.abs(jnp.linspace(min_decay, max_decay, HY_WIDTH, dtype=F32))[None, :]
    cos_t, sin_t = _rope_lane_tables(seq)
    w_in_r = _relayout_w_in(w_in)
    wq, wk, wv, onescol = _relayout_mla(mla_w_uq, mla_w_ukv)
    wg_lru, bg_lru = _relayout_lru_gates(lru_wa, lru_wx, lru_ba, lru_bx)
    w1p = jnp.pad(hy_ffn_w1, ((0, 0), (0, LANE - HY_POS_EMB), (0, 0)))
    rw_p = jnp.pad(router_w, ((0, 0), (0, 0), (0, LANE - N_EXPERTS)))
    w_out_b = w_out.astype(BF16)

    ar, ai, any_ = _hyena_filter_call(ztab, w1p, hy_ffn_b1[:, None, :], hy_sin_f1[:, None, :], hy_ffn_w2,
                                      hy_ffn_b2[:, None, :], hy_sin_f2[:, None, :], hy_ffn_w3, deltas,
                                      c32, s32, seq)
    skip = hy_skip[:, :, None, :]

    xc = _ln_call(x.reshape(t, d), row(ln_in_g), row(ln_in_b), tm)
    for l in range(depth):
        proj = _proj_call(xc, w_in_r[l], tm).reshape(bsz, seq, PROJ_W)
        y_hy = _hyena_call(proj, hy_conv_w[l], row(hy_conv_b[l]), cbf, sbf, ar, ai, any_, skip, l)
        y_lr = _lru_call(proj, lru_conv_w[l], row(lru_conv_b[l]), wg_lru[l], bg_lru[l],
                         lru_lambda[l].reshape(1, -1))
        q, k, v = _mla_prep_call(proj, row(mla_q_norm_g[l]), wq[l], row(mla_kv_norm_g[l]), wk[l], wv[l],
                                 onescol, cos_t, sin_t)
        y_at = _attn_call(q, k, v)
        x1, x1b = _postmix_call(y_hy.reshape(t, -1), y_lr.reshape(t, -1), y_at.reshape(t, -1), xc,
                                row(group_norm_g[l]), w_out_b[l], row(ln_mix_g[l]), row(ln_mix_b[l]), tm)
        x1 = x1.reshape(bsz, seq, d)
        pos, post, gatet = _select_call(x1, rw_p[l], cap)
        xe = _gather_call(pos, x1b.reshape(bsz, seq, d), cap)
        ye = _ffn_call(xe, exp_w_gate, exp_w_up, exp_w_down, l, cap)
        xc = _combine_call(post, gatet, ye, x1, row(ln_ffn_g[l]), row(ln_ffn_b[l]), cap).reshape(t, d)
    return xc.reshape(bsz, seq, d)
```

```python
import functools
import math

import jax
import jax.numpy as jnp
from jax import lax
from jax.experimental import pallas as pl
from jax.experimental.pallas import tpu as pltpu

F32 = jnp.float32
BF16 = jnp.bfloat16
HIGHEST = lax.Precision.HIGHEST

D_MODEL = 1024
HY_WIDTH = 256
LRU_WIDTH = 256
LRU_BLOCKS = 4
MLA_HEADS = 8
MLA_NOPE = 64
MLA_ROPE = 32
MLA_V = 64
MLA_Q_LORA = 768
MLA_KV_LORA = 256
ROPE_THETA = 10000.0
HY_POS_EMB = 33
HY_FILTER_FFN = 64
HY_FAST_DECAY = 0.3
HY_SLOW_DECAY = 1.5
HY_DECAY_TARGET = 1e-2
LRU_C = 8.0
N_EXPERTS = 16
EC_CAPACITY_FACTOR = 2
EPS = 1e-5
DEPTH = 4
ALPHA = (2.0 * DEPTH) ** 0.25

LANE = 128
MXU_N = 256
SUBLANES = 8
ROW_CHUNK = 256
HY_FREQ_BLOCK = 256
HY_LOOKAHEAD = 2
ATTN_COL_TILE = 256
ATTN_LOOKAHEAD = 3
HEAD_PAD = 128
PROJ_W = 2432
COL_HY, COL_CQ, COL_LR, COL_LG, COL_CKV, COL_KR = 0, 768, 1536, 1792, 2048, 2304
KR_LANE = 64
VMEM_CAP = 64 * 1024 * 1024


def _cparams(sem, vmem_mb):
    return pltpu.CompilerParams(dimension_semantics=sem, vmem_limit_bytes=vmem_mb * 1024 * 1024)


def _const_spec(shape):
    nd = len(shape)
    return pl.BlockSpec(shape, lambda *_: (0,) * nd, pipeline_mode=pl.Buffered(1))


def _layer_norm(x, g, b):
    xc = x - jnp.mean(x, axis=-1, keepdims=True)
    var = jnp.mean(xc * xc, axis=-1, keepdims=True)
    return xc * lax.rsqrt(var + EPS) * g + b


def _rms_norm(x, g):
    return x * lax.rsqrt(jnp.mean(x * x, axis=-1, keepdims=True) + EPS) * g


def _chunk_shift(load_row, x, r0, d, seq):
    rc = x.shape[0]
    if d == 0:
        return x
    t = lax.broadcasted_iota(jnp.int32, x.shape, 0)
    y = pltpu.roll(x, d % rc, axis=0)
    for k in range(abs(d)):
        dst = k if d > 0 else rc + d + k
        src = r0 - d + dst
        row = load_row(src) if 0 <= src < seq else jnp.zeros((1, x.shape[1]), x.dtype)
        y = jnp.where(t == dst, row, y)
    return y


def _sigmoid(x):
    return 1.0 / (1.0 + jnp.exp(-x))


def _gelu_tanh(x):
    c = math.sqrt(2.0 / math.pi)
    return 0.5 * x * (1.0 + jnp.tanh(c * (x + 0.044715 * (x * x * x))))


def _ln_kernel(x_ref, g_ref, b_ref, o_ref):
    o_ref[...] = _layer_norm(x_ref[...], g_ref[...], b_ref[...])


def _ln_call(x2d, g, b, tm):
    t, d = x2d.shape
    return pl.pallas_call(
        _ln_kernel,
        out_shape=jax.ShapeDtypeStruct((t, d), F32),
        grid=(t // tm,),
        in_specs=[pl.BlockSpec((tm, d), lambda i: (i, 0)), _const_spec((1, d)), _const_spec((1, d))],
        out_specs=pl.BlockSpec((tm, d), lambda i: (i, 0)),
        compiler_params=_cparams(("parallel",), 32),
        name="ln_in",
    )(x2d, g, b)


def _proj_kernel(x_ref, w_ref, o_ref):
    o_ref[...] = jnp.dot(x_ref[...].astype(BF16), w_ref[...], preferred_element_type=F32)


def _proj_call(x2d, w, tm):
    t, d = x2d.shape
    n = w.shape[1]
    return pl.pallas_call(
        _proj_kernel,
        out_shape=jax.ShapeDtypeStruct((t, n), F32),
        grid=(t // tm,),
        in_specs=[pl.BlockSpec((tm, d), lambda i: (i, 0)), _const_spec((d, n))],
        out_specs=pl.BlockSpec((tm, n), lambda i: (i, 0)),
        compiler_params=_cparams(("parallel",), 48),
        name="proj",
    )(x2d, w)


def _split_bf16(x):
    hi = x.astype(BF16)
    return hi, (x - hi.astype(F32)).astype(BF16)


def _dft_table_kernel(chi_ref, shi_ref, clo_ref, slo_ref, tc_scr, ts_scr, *, seq, rb):
    j = pl.program_id(0)
    mask = 2 * seq - 1
    step = math.pi / seq

    @pl.when(j == 0)
    def _():
        fi = lax.broadcasted_iota(jnp.int32, (rb, seq), 0)
        t = lax.broadcasted_iota(jnp.int32, (rb, seq), 1)
        ang = ((fi * t) & mask).astype(F32) * step
        tc_scr[...] = jnp.cos(ang)
        ts_scr[...] = jnp.sin(ang)

    t = lax.broadcasted_iota(jnp.int32, (SUBLANES, seq), 1)
    ang = (((j * rb) * t) & mask).astype(F32) * step
    ca = jnp.cos(ang)[0:1, :]
    sa = jnp.sin(ang)[0:1, :]
    c = ca * tc_scr[...] - sa * ts_scr[...]
    s = sa * tc_scr[...] + ca * ts_scr[...]
    chi_ref[...], clo_ref[...] = _split_bf16(c)
    shi_ref[...], slo_ref[...] = _split_bf16(s)


def _dft_table_call(seq):
    assert seq & (seq - 1) == 0, "sequence length must be a power of two"
    rb = min(256, seq)
    blk = pl.BlockSpec((rb, seq), lambda j: (j, 0))
    return pl.pallas_call(
        functools.partial(_dft_table_kernel, seq=seq, rb=rb),
        out_shape=tuple(jax.ShapeDtypeStruct((seq, seq), BF16) for _ in range(4)),
        grid=(seq // rb,),
        in_specs=[],
        out_specs=(blk, blk, blk, blk),
        scratch_shapes=[pltpu.VMEM((rb, seq), F32), pltpu.VMEM((rb, seq), F32)],
        compiler_params=_cparams(("arbitrary",), 40),
        name="dft_tables",
    )()


def _hyena_filter_kernel(z_ref, w1_ref, b1_ref, f1_ref, w2_ref, b2_ref, f2_ref, w3_ref, dl_ref,
                         chi_ref, shi_ref, clo_ref, slo_ref, ar_ref, ai_ref, any_ref,
                         hs_hi, hs_lo, hd_hi, hd_lo, *, seq, fb):
    j = pl.program_id(1)
    n_fft = 2 * seq

    @pl.when(j == 0)
    def _():
        z = z_ref[...]
        h = jnp.sin(f1_ref[0] * (jnp.dot(z, w1_ref[0], precision=HIGHEST, preferred_element_type=F32)
                                 + b1_ref[0]))
        h = jnp.sin(f2_ref[0] * (jnp.dot(h, w2_ref[0], precision=HIGHEST, preferred_element_type=F32)
                                 + b2_ref[0]))
        t_idx = lax.broadcasted_iota(jnp.int32, (seq, HY_WIDTH), 0)
        t = t_idx.astype(F32) * (1.0 / (seq - 1))
        window = jnp.exp(-t * dl_ref[...])
        sign = jnp.where(t_idx % 2 == 0, 1.0, -1.0)

        def filt(g):
            w3g = w3_ref[0, :, g * HY_WIDTH:(g + 1) * HY_WIDTH]
            return jnp.dot(h, w3g, precision=HIGHEST, preferred_element_type=F32) * window

        for o in range(2):
            hf = filt(o)
            hb = filt(2 + o)
            hb0 = jnp.where(t_idx == 0, 0.0, hb)
            hs = hf + hb0
            cols = slice(o * HY_WIDTH, (o + 1) * HY_WIDTH)
            hs_hi[:, cols], hs_lo[:, cols] = _split_bf16(hs)
            hd_hi[:, cols], hd_lo[:, cols] = _split_bf16(hb0 - hf)
            any_ref[0, o] = jnp.sum(hs * sign, axis=0, keepdims=True) * (1.0 / n_fft)

    def dot3(t_hi, t_lo, h_hi, h_lo):
        return (jnp.dot(t_hi[...], h_hi[...], preferred_element_type=F32)
                + (jnp.dot(t_hi[...], h_lo[...], preferred_element_type=F32)
                   + jnp.dot(t_lo[...], h_hi[...], preferred_element_type=F32)))

    kr = dot3(chi_ref, clo_ref, hs_hi, hs_lo)
    ki = dot3(shi_ref, slo_ref, hd_hi, hd_lo)
    f_idx = lax.broadcasted_iota(jnp.int32, (fb, 2 * HY_WIDTH), 0) + j * fb
    wf = jnp.where(f_idx == 0, 1.0 / n_fft, 2.0 / n_fft)
    kr = kr * wf
    ki = ki * wf
    for o in range(2):
        ar_ref[0, o] = kr[:, o * HY_WIDTH:(o + 1) * HY_WIDTH]
        ai_ref[0, o] = ki[:, o * HY_WIDTH:(o + 1) * HY_WIDTH]


def _hyena_filter_call(ztab, w1, b1, f1, w2, b2, f2, w3, deltas, chi, shi, clo, slo, seq):
    depth = w1.shape[0]
    fb = min(512, seq)
    tab = pl.BlockSpec((fb, seq), lambda l, j: (j, 0))
    nf = seq // fb
    kern = functools.partial(_hyena_filter_kernel, seq=seq, fb=fb)
    lay = lambda l, j: (l, 0, 0)
    return pl.pallas_call(
        kern,
        out_shape=(jax.ShapeDtypeStruct((depth, 2, seq, HY_WIDTH), F32),
                   jax.ShapeDtypeStruct((depth, 2, seq, HY_WIDTH), F32),
                   jax.ShapeDtypeStruct((depth, 2, 1, HY_WIDTH), F32)),
        grid=(depth, nf),
        in_specs=[
            _const_spec(ztab.shape),
            pl.BlockSpec((1,) + w1.shape[1:], lay), pl.BlockSpec((1,) + b1.shape[1:], lay),
            pl.BlockSpec((1,) + f1.shape[1:], lay),
            pl.BlockSpec((1,) + w2.shape[1:], lay), pl.BlockSpec((1,) + b2.shape[1:], lay),
            pl.BlockSpec((1,) + f2.shape[1:], lay),
            pl.BlockSpec((1,) + w3.shape[1:], lay),
            _const_spec(deltas.shape), tab, tab, tab, tab,
        ],
        out_specs=(pl.BlockSpec((1, 2, fb, HY_WIDTH), lambda l, j: (l, 0, j, 0)),
                   pl.BlockSpec((1, 2, fb, HY_WIDTH), lambda l, j: (l, 0, j, 0)),
                   pl.BlockSpec((1, 2, 1, HY_WIDTH), lambda l, j: (l, 0, 0, 0))),
        scratch_shapes=[pltpu.VMEM((seq, 2 * HY_WIDTH), BF16) for _ in range(4)],
        compiler_params=_cparams(("arbitrary", "arbitrary"), 48),
        name="hyena_filter",
    )(ztab, w1, b1, f1, w2, b2, f2, w3, deltas, chi, shi, clo, slo)


def _hyena_kernel(p_ref, cw_ref, cb_ref, c_ref, s_ref, ar_ref, ai_ref, any_ref, sk_ref, o_ref,
                  g_scr, z_scr, zb_scr, acc_scr, *, seq, fb):
    w = HY_WIDTH
    rc = min(ROW_CHUNK, seq)
    load_row = lambda r: p_ref[0, r:r + 1, :]
    for r0 in range(0, seq, rc):
        x = p_ref[0, r0:r0 + rc, :]
        u = (cw_ref[0:1, :] * _chunk_shift(load_row, x, r0, 1, seq) + cw_ref[1:2, :] * x
             + cw_ref[2:3, :] * _chunk_shift(load_row, x, r0, -1, seq) + cb_ref[...])
        g_scr[r0:r0 + rc, :] = u[:, 0:2 * w]
        z_scr[r0:r0 + rc, :] = u[:, 2 * w:3 * w]
    t_idx = lax.broadcasted_iota(jnp.int32, (seq, w), 0)
    sign = jnp.where(t_idx % 2 == 0, 1.0, -1.0)
    for o in range(2):
        z = z_scr[...]
        zb_scr[...] = z.astype(BF16)
        acc_scr[...] = sign * (jnp.sum(z * sign, axis=0, keepdims=True) * any_ref[0, o])
        nf = seq // fb

        def forward(j):
            rows = slice(j * fb, (j + 1) * fb)
            return (jnp.dot(c_ref[rows, :], zb_scr[...], preferred_element_type=F32),
                    jnp.dot(s_ref[rows, :], zb_scr[...], preferred_element_type=F32))

        pending = [forward(j) for j in range(min(HY_LOOKAHEAD, nf))]
        for j in range(nf):
            if j + HY_LOOKAHEAD < nf:
                pending.append(forward(j + HY_LOOKAHEAD))
            ur, us = pending.pop(0)
            rows = slice(j * fb, (j + 1) * fb)
            ar = ar_ref[0, o, rows, :]
            ai = ai_ref[0, o, rows, :]
            pp = (ur * ar + us * ai).astype(BF16)
            qq = (us * ar - ur * ai).astype(BF16)
            acc_scr[...] += (jnp.dot(c_ref[:, rows], pp, preferred_element_type=F32)
                             + jnp.dot(s_ref[:, rows], qq, preferred_element_type=F32))
        z_new = g_scr[:, o * w:(o + 1) * w] * (acc_scr[...] + sk_ref[0, o] * z_scr[...])
        if o == 0:
            z_scr[...] = z_new
        else:
            o_ref[0] = z_new


def _hyena_call(proj, cw, cb, cbf, sbf, ar, ai, any_, skip, layer):
    bsz, seq, _ = proj.shape
    fb = min(HY_FREQ_BLOCK, seq)
    kern = functools.partial(_hyena_kernel, seq=seq, fb=fb)
    lsel = lambda b: (layer, 0, 0, 0)
    return pl.pallas_call(
        kern,
        out_shape=jax.ShapeDtypeStruct((bsz, seq, HY_WIDTH), F32),
        grid=(bsz,),
        in_specs=[
            pl.BlockSpec((1, seq, 3 * HY_WIDTH), lambda b: (b, 0, COL_HY // (3 * HY_WIDTH)),
                         pipeline_mode=pl.Buffered(1)),
            _const_spec(cw.shape), _const_spec(cb.shape),
            _const_spec(cbf.shape), _const_spec(sbf.shape),
            pl.BlockSpec((1, 2, seq, HY_WIDTH), lsel, pipeline_mode=pl.Buffered(1)),
            pl.BlockSpec((1, 2, seq, HY_WIDTH), lsel, pipeline_mode=pl.Buffered(1)),
            pl.BlockSpec((1, 2, 1, HY_WIDTH), lsel, pipeline_mode=pl.Buffered(1)),
            pl.BlockSpec((1, 2, 1, HY_WIDTH), lsel, pipeline_mode=pl.Buffered(1)),
        ],
        out_specs=pl.BlockSpec((1, seq, HY_WIDTH), lambda b: (b, 0, 0)),
        scratch_shapes=[pltpu.VMEM((seq, 2 * HY_WIDTH), F32), pltpu.VMEM((seq, HY_WIDTH), F32),
                        pltpu.VMEM((seq, HY_WIDTH), BF16), pltpu.VMEM((seq, HY_WIDTH), F32)],
        compiler_params=_cparams(("parallel",), 60),
        name="hyena",
    )(proj, cw, cb, cbf, sbf, ar, ai, any_, skip)


def _lru_kernel(xr_ref, xg_ref, cw_ref, cb_ref, wg_ref, bg_ref, lam_ref, o_ref, a_scr, u_scr, *, seq):
    w = LRU_WIDTH
    rc = min(ROW_CHUNK, seq)
    lam = lam_ref[...]
    y = jnp.exp(-jnp.abs(lam))
    w1 = 1.0 + y
    log1p_y = jnp.where(w1 == 1.0, y, jnp.log(w1) * (y / jnp.where(w1 == 1.0, 1.0, w1 - 1.0)))
    softplus_neg = jnp.maximum(-lam, 0.0) + log1p_y
    load_row = lambda r: xr_ref[0, r:r + 1, :]
    t_in = lax.broadcasted_iota(jnp.int32, (rc, w), 0) % SUBLANES
    for r0 in range(0, seq, rc):
        x = xr_ref[0, r0:r0 + rc, :]
        xc = (cw_ref[0:1, :] * _chunk_shift(load_row, x, r0, 2, seq)
              + cw_ref[1:2, :] * _chunk_shift(load_row, x, r0, 1, seq)
              + cw_ref[2:3, :] * x + cw_ref[3:4, :] * _chunk_shift(load_row, x, r0, -1, seq) + cb_ref[...])
        xcb = xc.astype(BF16)
        for d in range(2):
            ca, cx = slice(d * w, (d + 1) * w), slice((2 + d) * w, (3 + d) * w)
            gate_a = _sigmoid(jnp.dot(xcb, wg_ref[:, ca], preferred_element_type=F32) + bg_ref[:, ca])
            gate_x = _sigmoid(jnp.dot(xcb, wg_ref[:, cx], preferred_element_type=F32) + bg_ref[:, cx])
            a = jnp.exp(-LRU_C * gate_a * softplus_neg[:, ca])
            u = jnp.sqrt(1.0 - a * a) * gate_x * xc
            s = 1
            while s < SUBLANES:
                keep = (t_in >= s) if d == 0 else (t_in < SUBLANES - s)
                sh = s if d == 0 else rc - s
                u = u + a * jnp.where(keep, pltpu.roll(u, sh, axis=0), 0.0)
                a = a * jnp.where(keep, pltpu.roll(a, sh, axis=0), 1.0)
                s *= 2
            a_scr[d, r0:r0 + rc, :] = a
            u_scr[d, r0:r0 + rc, :] = u

    ng = seq // SUBLANES

    def carry_step(i, carry):
        cf, cb = carry
        rf = pl.multiple_of(i * SUBLANES, SUBLANES)
        rb = pl.multiple_of((ng - 1 - i) * SUBLANES, SUBLANES)
        hf = u_scr[0, pl.ds(rf, SUBLANES), :] + a_scr[0, pl.ds(rf, SUBLANES), :] * cf
        hb = u_scr[1, pl.ds(rb, SUBLANES), :] + a_scr[1, pl.ds(rb, SUBLANES), :] * cb
        u_scr[0, pl.ds(rf, SUBLANES), :] = hf
        u_scr[1, pl.ds(rb, SUBLANES), :] = hb
        return hf[SUBLANES - 1:SUBLANES, :], hb[0:1, :]

    zero = jnp.zeros((1, w), F32)
    lax.fori_loop(0, ng, carry_step, (zero, zero), unroll=4)
    for r0 in range(0, seq, rc):
        rows = slice(r0, r0 + rc)
        o_ref[0, rows, :] = (u_scr[0, rows, :] + u_scr[1, rows, :]) * _gelu_tanh(xg_ref[0, rows, :])


def _lru_call(proj, cw, cb, wg, bg, lam):
    bsz, seq, _ = proj.shape
    w = LRU_WIDTH
    return pl.pallas_call(
        functools.partial(_lru_kernel, seq=seq),
        scratch_shapes=[pltpu.VMEM((2, seq, w), F32), pltpu.VMEM((2, seq, w), F32)],
        out_shape=jax.ShapeDtypeStruct((bsz, seq, w), F32),
        grid=(bsz,),
        in_specs=[
            pl.BlockSpec((1, seq, w), lambda b: (b, 0, COL_LR // w)),
            pl.BlockSpec((1, seq, w), lambda b: (b, 0, COL_LG // w)),
            _const_spec(cw.shape), _const_spec(cb.shape), _const_spec(wg.shape),
            _const_spec(bg.shape), _const_spec(lam.shape),
        ],
        out_specs=pl.BlockSpec((1, seq, w), lambda b: (b, 0, 0)),
        compiler_params=_cparams(("parallel",), 48),
        name="rglru",
    )(proj, proj, cw, cb, wg, bg, lam)


def _rope128(x, cos, sin):
    lane = lax.broadcasted_iota(jnp.int32, x.shape, 1)
    half = MLA_ROPE // 2
    partner = jnp.where(lane < KR_LANE + half, pltpu.roll(x, LANE - half, axis=1),
                        pltpu.roll(x, half, axis=1))
    return x * cos + partner * sin


def _mla_prep_kernel(cq_ref, ckv_ref, kr_ref, qg_ref, wq_ref, kg_ref, wk_ref, wvt_ref, cos_ref,
                     sin_ref, q_ref, k_ref, vt_ref, *, qscale):
    cos = cos_ref[...]
    sin = sin_ref[...]
    qn = _rms_norm(cq_ref[0], qg_ref[...]).astype(BF16)
    q = jnp.dot(qn, wq_ref[...], preferred_element_type=F32)
    kvn = _rms_norm(ckv_ref[0], kg_ref[...]).astype(BF16)
    kn = jnp.dot(kvn, wk_ref[...], preferred_element_type=F32)
    v_t = lax.dot_general(wvt_ref[...], kvn, (((1,), (1,)), ((), ())), preferred_element_type=F32)
    row = lax.broadcasted_iota(jnp.int32, v_t.shape, 0)
    vt_ref[0] = jnp.where((row & (HEAD_PAD - 1)) == MLA_V, 1.0, v_t).astype(BF16)
    kr = _rope128(kr_ref[0], cos, sin)
    for h in range(MLA_HEADS):
        cols = slice(h * HEAD_PAD, (h + 1) * HEAD_PAD)
        q_ref[0, :, cols] = (_rope128(q[:, cols], cos, sin) * qscale).astype(BF16)
        k_ref[0, :, cols] = (kn[:, cols] + kr).astype(BF16)


def _mla_prep_call(proj, qg, wq, kg, wk, wvt, cos_t, sin_t):
    bsz, seq, _ = proj.shape
    hp = MLA_HEADS * HEAD_PAD
    tr = min(512, seq)
    qscale = float((MLA_NOPE + MLA_ROPE) ** -0.5 * math.log2(math.e))
    return pl.pallas_call(
        functools.partial(_mla_prep_kernel, qscale=qscale),
        out_shape=(jax.ShapeDtypeStruct((bsz, seq, hp), BF16),
                   jax.ShapeDtypeStruct((bsz, seq, hp), BF16),
                   jax.ShapeDtypeStruct((bsz, hp, seq), BF16)),
        grid=(bsz, seq // tr),
        in_specs=[
            pl.BlockSpec((1, tr, MLA_Q_LORA), lambda b, i: (b, i, COL_CQ // MLA_Q_LORA)),
            pl.BlockSpec((1, tr, MLA_KV_LORA), lambda b, i: (b, i, COL_CKV // MLA_KV_LORA)),
            pl.BlockSpec((1, tr, LANE), lambda b, i: (b, i, COL_KR // LANE)),
            _const_spec(qg.shape), _const_spec(wq.shape), _const_spec(kg.shape),
            _const_spec(wk.shape), _const_spec(wvt.shape),
            pl.BlockSpec((tr, LANE), lambda b, i: (i, 0)), pl.BlockSpec((tr, LANE), lambda b, i: (i, 0)),
        ],
        out_specs=(pl.BlockSpec((1, tr, hp), lambda b, i: (b, i, 0)),
                   pl.BlockSpec((1, tr, hp), lambda b, i: (b, i, 0)),
                   pl.BlockSpec((1, hp, tr), lambda b, i: (b, 0, i))),
        compiler_params=_cparams(("parallel", "parallel"), 48),
        name="mla_prep",
    )(proj, proj, proj, qg, wq, kg, wk, wvt, cos_t, sin_t)


def _attn_kernel(q_ref, k_ref, vt_ref, o_ref, s_scr, p_scr):
    tq = q_ref.shape[1]
    ct = min(ATTN_COL_TILE, tq)
    chains = [(h, c0) for c0 in range(0, tq, ct) for h in range(2)]
    nt = (((1,), (1,)), ((), ()))

    ns = s_scr.shape[0]

    def scores(i):
        h, c0 = chains[i]
        cols = slice(h * HEAD_PAD, (h + 1) * HEAD_PAD)
        s_scr[i % ns] = lax.dot_general(k_ref[0, :, cols], q_ref[0, c0:c0 + ct, cols], nt,
                                        preferred_element_type=F32)

    def finish(i):
        h = chains[i][0]
        s_t = s_scr[i % ns]
        p_scr[i % 2] = jnp.exp2(s_t - jnp.max(s_t, axis=0, keepdims=True)).astype(BF16)
        o_t = jnp.dot(vt_ref[0, h * HEAD_PAD:(h + 1) * HEAD_PAD, :], p_scr[i % 2],
                      preferred_element_type=F32)
        return o_t[0:MLA_V, :] / o_t[MLA_V:MLA_V + 1, :]

    for i in range(min(ATTN_LOOKAHEAD, len(chains))):
        scores(i)
    done = {}
    for i, (h, c0) in enumerate(chains):
        if i + ATTN_LOOKAHEAD < len(chains):
            scores(i + ATTN_LOOKAHEAD)
        done[h] = finish(i)
        if h == 1:
            o_ref[0, c0:c0 + ct, :] = jnp.concatenate([done[0], done[1]], axis=0).T


def _attn_call(q, k, vt):
    bsz, seq, hp = q.shape
    tq = min(1024, seq)
    pw = 2 * HEAD_PAD
    return pl.pallas_call(
        _attn_kernel,
        out_shape=jax.ShapeDtypeStruct((bsz, seq, MLA_HEADS * MLA_V), F32),
        grid=(bsz, MLA_HEADS // 2, seq // tq),
        in_specs=[pl.BlockSpec((1, tq, pw), lambda b, j, i: (b, i, j)),
                  pl.BlockSpec((1, seq, pw), lambda b, j, i: (b, 0, j)),
                  pl.BlockSpec((1, pw, seq), lambda b, j, i: (b, j, 0))],
        out_specs=pl.BlockSpec((1, tq, HEAD_PAD), lambda b, j, i: (b, i, j)),
        scratch_shapes=[pltpu.VMEM((ATTN_LOOKAHEAD + 1, seq, min(ATTN_COL_TILE, tq)), F32),
                        pltpu.VMEM((2, seq, min(ATTN_COL_TILE, tq)), BF16)],
        compiler_params=_cparams(("parallel", "parallel", "parallel"), 48),
        name="mla_attn",
    )(q, k, vt)


def _postmix_kernel(yh_ref, yl_ref, ya_ref, x_ref, gn_ref, wo_ref, g_ref, b_ref, o_ref, ob_ref):
    g0, g1 = HY_WIDTH, HY_WIDTH + LRU_WIDTH
    y = jnp.concatenate([_rms_norm(yh_ref[...], gn_ref[:, 0:g0]),
                         _rms_norm(yl_ref[...], gn_ref[:, g0:g1]),
                         _rms_norm(ya_ref[...], gn_ref[:, g1:])], axis=-1).astype(BF16)
    mix = jnp.dot(y, wo_ref[...], preferred_element_type=F32)
    x1 = _layer_norm(ALPHA * x_ref[...] + mix, g_ref[...], b_ref[...])
    o_ref[...] = x1
    ob_ref[...] = x1.astype(BF16)


def _postmix_call(yh, yl, ya, x2d, gn, wo, g, b, tm):
    t, d = x2d.shape
    row = lambda w: pl.BlockSpec((tm, w), lambda i: (i, 0))
    return pl.pallas_call(
        _postmix_kernel,
        out_shape=(jax.ShapeDtypeStruct((t, d), F32), jax.ShapeDtypeStruct((t, d), BF16)),
        grid=(t // tm,),
        in_specs=[row(HY_WIDTH), row(LRU_WIDTH), row(MLA_HEADS * MLA_V), row(d),
                  _const_spec(gn.shape), _const_spec(wo.shape), _const_spec(g.shape), _const_spec(b.shape)],
        out_specs=(row(d), row(d)),
        compiler_params=_cparams(("parallel",), 48),
        name="postmix",
    )(yh, yl, ya, x2d, gn, wo, g, b)


def _cumsum_lanes(x):
    n = x.shape[1]
    lane = lax.broadcasted_iota(jnp.int32, x.shape, 1)
    d = 1
    while d < n:
        x = x + jnp.where(lane >= d, pltpu.roll(x, d, axis=1), 0.0)
        d *= 2
    return x


def _select_kernel(x_ref, rw_ref, pos_ref, post_ref, gatet_ref, *, cap):
    x = x_ref[0]
    x_hi, x_lo = _split_bf16(x)
    w_hi, w_lo = _split_bf16(rw_ref[...])
    logits = (jnp.dot(x_hi, w_hi, preferred_element_type=F32)
              + (jnp.dot(x_hi, w_lo, preferred_element_type=F32)
                 + jnp.dot(x_lo, w_hi, preferred_element_type=F32)))
    lane = lax.broadcasted_iota(jnp.int32, logits.shape, 1)
    valid = lane < N_EXPERTS
    logits = jnp.where(valid, logits, -1e30)
    m = jnp.max(logits, axis=-1, keepdims=True)
    ex = jnp.where(valid, jnp.exp(logits - m), 0.0)
    aff = ex / jnp.sum(ex, axis=-1, keepdims=True)
    seq = aff.shape[0]
    aff_e = aff.T[0:N_EXPERTS, :]

    def body(_, carry):
        lo, hi = carry
        mid = lo + ((hi - lo + 1) >> 1)
        cnt = jnp.sum(jnp.where(aff_e >= pltpu.bitcast(mid, F32), 1.0, 0.0), axis=1, keepdims=True)
        ok = cnt >= float(cap)
        return jnp.where(ok, mid, lo), jnp.where(ok, hi, mid - 1)

    lo0 = jnp.zeros((N_EXPERTS, 1), jnp.int32)
    hi0 = jnp.full((N_EXPERTS, 1), 0x7F7FFFFF, jnp.int32)
    thr_bits, _ = lax.fori_loop(0, 31, body, (lo0, hi0))
    thr = pltpu.bitcast(thr_bits, F32)
    gt = aff_e > thr
    eq = aff_e == thr
    need = float(cap) - jnp.sum(jnp.where(gt, 1.0, 0.0), axis=1, keepdims=True)
    eq_rank = _cumsum_lanes(jnp.where(eq, 1.0, 0.0))
    sel = gt | (eq & (eq_rank <= need))
    pos = jnp.where(sel, _cumsum_lanes(jnp.where(sel, 1.0, 0.0)) - 1.0, -1.0)
    pos_ref[0] = pos
    fill = jnp.full((LANE - N_EXPERTS, seq), -1.0, F32)
    post_ref[0] = jnp.concatenate([pos, fill], axis=0).T
    gatet_ref[0] = jnp.concatenate([jnp.where(sel, aff_e, 0.0), fill], axis=0).T


def _select_call(x1, rw, cap):
    bsz, seq, d = x1.shape
    kern = functools.partial(_select_kernel, cap=cap)
    return pl.pallas_call(
        kern,
        out_shape=(jax.ShapeDtypeStruct((bsz, N_EXPERTS, seq), F32),
                   jax.ShapeDtypeStruct((bsz, seq, LANE), F32),
                   jax.ShapeDtypeStruct((bsz, seq, LANE), F32)),
        grid=(bsz,),
        in_specs=[pl.BlockSpec((1, seq, d), lambda b: (b, 0, 0)), _const_spec(rw.shape)],
        out_specs=(pl.BlockSpec((1, N_EXPERTS, seq), lambda b: (b, 0, 0)),
                   pl.BlockSpec((1, seq, LANE), lambda b: (b, 0, 0)),
                   pl.BlockSpec((1, seq, LANE), lambda b: (b, 0, 0))),
        compiler_params=_cparams(("parallel",), 48),
        name="moe_select",
    )(x1, rw)


def _gather_kernel(pos_ref, x_ref, o_ref, p_scr, *, cap):
    seq, d = x_ref.shape[1], x_ref.shape[2]
    slot = lax.broadcasted_iota(jnp.int32, (cap, seq), 0).astype(F32)
    for e in range(N_EXPERTS):
        p_scr[e * cap:(e + 1) * cap, :] = jnp.where(pos_ref[0, e:e + 1, :] == slot, 1.0, 0.0).astype(BF16)
    cw = min(MXU_N, d)
    for c in range(0, d, cw):
        o_ref[0, :, c:c + cw] = jnp.dot(p_scr[...], x_ref[0, :, c:c + cw],
                                        preferred_element_type=F32).astype(BF16)


def _gather_call(pos, x1b, cap):
    bsz, seq, d = x1b.shape
    return pl.pallas_call(
        functools.partial(_gather_kernel, cap=cap),
        out_shape=jax.ShapeDtypeStruct((bsz, N_EXPERTS * cap, d), BF16),
        grid=(bsz,),
        in_specs=[pl.BlockSpec((1, N_EXPERTS, seq), lambda b: (b, 0, 0)),
                  pl.BlockSpec((1, seq, d), lambda b: (b, 0, 0))],
        out_specs=pl.BlockSpec((1, N_EXPERTS * cap, d), lambda b: (b, 0, 0)),
        scratch_shapes=[pltpu.VMEM((N_EXPERTS * cap, seq), BF16)],
        compiler_params=_cparams(("parallel",), 56),
        name="moe_gather",
    )(pos, x1b)


def _ffn_kernel(xe_ref, wg_ref, wu_ref, wd_ref, o_ref, acc_scr, *, rows_per_dot):
    f = pl.program_id(1)
    bsz, cap, d = xe_ref.shape
    wg = wg_ref[0, 0].astype(BF16)
    wu = wu_ref[0, 0].astype(BF16)
    wd = wd_ref[0, 0].astype(BF16)
    nb = rows_per_dot // cap

    @pl.when(f == 0)
    def _():
        acc_scr[...] = jnp.zeros_like(acc_scr)

    def up(b0):
        xe = xe_ref[b0:b0 + nb].reshape(nb * cap, d)
        return (jnp.dot(xe, wg, preferred_element_type=F32), jnp.dot(xe, wu, preferred_element_type=F32))

    starts = list(range(0, bsz, nb))
    nxt = up(starts[0])
    for i, b0 in enumerate(starts):
        hg, hu = nxt
        if i + 1 < len(starts):
            nxt = up(starts[i + 1])
        hid = (hg * _sigmoid(hg) * hu).astype(BF16)
        rows = slice(b0 * cap, (b0 + nb) * cap)
        acc_scr[rows, :] += jnp.dot(hid, wd, preferred_element_type=F32)

    @pl.when(f == pl.num_programs(1) - 1)
    def _():
        o_ref[...] = acc_scr[...].reshape(bsz, cap, d).astype(BF16)


def _ffn_call(xe, wg, wu, wd, layer, cap):
    bsz, _, d = xe.shape
    _, ne, _, ff = wg.shape
    fcw = min(512, ff)
    rows_per_dot = cap * max(1, min(bsz, 1024 // cap))
    return pl.pallas_call(
        functools.partial(_ffn_kernel, rows_per_dot=rows_per_dot),
        out_shape=jax.ShapeDtypeStruct((bsz, ne * cap, d), BF16),
        grid=(ne, ff // fcw),
        in_specs=[pl.BlockSpec((bsz, cap, d), lambda e, f: (0, e, 0)),
                  pl.BlockSpec((1, 1, d, fcw), lambda e, f: (layer, e, 0, f)),
                  pl.BlockSpec((1, 1, d, fcw), lambda e, f: (layer, e, 0, f)),
                  pl.BlockSpec((1, 1, fcw, d), lambda e, f: (layer, e, f, 0))],
        out_specs=pl.BlockSpec((bsz, cap, d), lambda e, f: (0, e, 0)),
        scratch_shapes=[pltpu.VMEM((bsz * cap, d), F32)],
        compiler_params=_cparams(("parallel", "arbitrary"), 56),
        name="moe_ffn",
    )(xe, wg, wu, wd)


def _combine_kernel(post_ref, gatet_ref, ye_ref, x_ref, g_ref, b_ref, o_ref, *, cap):
    tr = x_ref.shape[1]
    slot = lax.broadcasted_iota(jnp.int32, (tr, cap), 1).astype(F32)
    post = post_ref[0]
    gatet = gatet_ref[0]
    acc = ALPHA * x_ref[0]
    for e in range(N_EXPERTS):
        scat = jnp.where(post[:, e:e + 1] == slot, gatet[:, e:e + 1], 0.0).astype(BF16)
        acc = acc + jnp.dot(scat, ye_ref[0, e * cap:(e + 1) * cap, :], preferred_element_type=F32)
    o_ref[0] = _layer_norm(acc, g_ref[...], b_ref[...])


def _combine_call(post, gatet, ye, x1, g, b, cap):
    bsz, seq, d = x1.shape
    tr = min(512, seq)
    kern = functools.partial(_combine_kernel, cap=cap)
    return pl.pallas_call(
        kern,
        out_shape=jax.ShapeDtypeStruct((bsz, seq, d), F32),
        grid=(bsz, seq // tr),
        in_specs=[pl.BlockSpec((1, tr, LANE), lambda bi, i: (bi, i, 0)),
                  pl.BlockSpec((1, tr, LANE), lambda bi, i: (bi, i, 0)),
                  pl.BlockSpec((1, N_EXPERTS * cap, d), lambda bi, i: (bi, 0, 0)),
                  pl.BlockSpec((1, tr, d), lambda bi, i: (bi, i, 0)),
                  _const_spec(g.shape), _const_spec(b.shape)],
        out_specs=pl.BlockSpec((1, tr, d), lambda bi, i: (bi, i, 0)),
        compiler_params=_cparams(("parallel", "parallel"), 48),
        name="moe_combine",
    )(post, gatet, ye, x1, g, b)


def _hyena_pos_table(seq):
    t = jnp.linspace(0.0, 1.0, seq, dtype=F32)[:, None]
    bands = (HY_POS_EMB - 1) // 2
    t_idx = jnp.arange(seq, dtype=F32)[:, None]
    freqs = jnp.linspace(1e-4, bands - 1, bands, dtype=F32)[None, :]
    ang = 2.0 * math.pi * t_idx * freqs / seq
    z = jnp.concatenate([t, jnp.cos(ang), -jnp.sin(ang)], axis=-1)
    return jnp.pad(z, ((0, 0), (0, LANE - HY_POS_EMB)))


def _rope_lane_tables(seq):
    half = MLA_ROPE // 2
    inv = ROPE_THETA ** (-jnp.arange(0, MLA_ROPE, 2, dtype=F32) / MLA_ROPE)
    ang = jnp.arange(seq, dtype=F32)[:, None] * inv[None, :]
    cos, sin = jnp.cos(ang), jnp.sin(ang)
    ones = jnp.ones((seq, KR_LANE), F32)
    zeros = jnp.zeros((seq, KR_LANE), F32)
    tail = LANE - KR_LANE - MLA_ROPE
    cos_t = jnp.concatenate([ones, cos, cos, jnp.ones((seq, tail), F32)], axis=-1)
    sin_t = jnp.concatenate([zeros, -sin, sin, jnp.zeros((seq, tail), F32)], axis=-1)
    return cos_t, sin_t


def _relayout_w_in(w_in):
    d = w_in.shape[1]
    hy, lr, lg, cq, ckv, kr = jnp.split(w_in, [768, 1024, 1280, 2048, 2304], axis=-1)
    z = lambda n: jnp.zeros(w_in.shape[:2] + (n,), w_in.dtype)
    out = jnp.concatenate([hy, cq, lr, lg, ckv, z(KR_LANE), kr, z(LANE - KR_LANE - MLA_ROPE)], axis=-1)
    assert out.shape[-1] == PROJ_W and d == D_MODEL
    return out.astype(BF16)


def _relayout_mla(w_uq, w_ukv):
    depth = w_uq.shape[0]
    qd = MLA_NOPE + MLA_ROPE
    wq = w_uq.reshape(depth, MLA_Q_LORA, MLA_HEADS, qd)
    wq = jnp.pad(wq, ((0, 0), (0, 0), (0, 0), (0, HEAD_PAD - qd)))
    wq = wq.reshape(depth, MLA_Q_LORA, MLA_HEADS * HEAD_PAD).astype(BF16)
    wkv = w_ukv.reshape(depth, MLA_KV_LORA, MLA_HEADS, MLA_NOPE + MLA_V)
    wk = jnp.pad(wkv[..., :MLA_NOPE], ((0, 0), (0, 0), (0, 0), (0, HEAD_PAD - MLA_NOPE)))
    wk = wk.reshape(depth, MLA_KV_LORA, MLA_HEADS * HEAD_PAD).astype(BF16)
    wv = jnp.pad(wkv[..., MLA_NOPE:], ((0, 0), (0, 0), (0, 0), (0, HEAD_PAD - MLA_V)))
    wvt = jnp.swapaxes(wv.reshape(depth, MLA_KV_LORA, MLA_HEADS * HEAD_PAD), 1, 2).astype(BF16)
    return wq, wk, wvt


def _relayout_lru_gates(wa, wx, ba, bx):
    depth = wa.shape[0]
    bw = LRU_WIDTH // LRU_BLOCKS
    eye = jnp.eye(LRU_BLOCKS, dtype=wa.dtype)

    def bd(w):
        full = jnp.einsum("ldnjk,nm->ldnjmk", w, eye)
        return full.reshape(depth, 2, LRU_WIDTH, LRU_WIDTH)

    a, x = bd(wa), bd(wx)
    wg = jnp.concatenate([a[:, 0], a[:, 1], x[:, 0], x[:, 1]], axis=-1).astype(BF16)
    bg = jnp.concatenate([ba[:, 0], ba[:, 1], bx[:, 0], bx[:, 1]], axis=-1)[:, None, :]
    del bw
    return wg, bg


def kernel(x, ln_in_g, ln_in_b, w_in, hy_conv_w, hy_conv_b, hy_ffn_w1, hy_ffn_b1, hy_sin_f1,
           hy_ffn_w2, hy_ffn_b2, hy_sin_f2, hy_ffn_w3, hy_skip, lru_conv_w, lru_conv_b, lru_wa, lru_ba,
           lru_wx, lru_bx, lru_lambda, mla_q_norm_g, mla_w_uq, mla_kv_norm_g, mla_w_ukv, group_norm_g,
           w_out, ln_mix_g, ln_mix_b, router_w, exp_w_gate, exp_w_up, exp_w_down, ln_ffn_g, ln_ffn_b):
    bsz, seq, d = x.shape
    depth = w_in.shape[0]
    t = bsz * seq
    tm = min(512, t)
    cap = max(1, EC_CAPACITY_FACTOR * seq // N_EXPERTS)
    row = lambda v: v.reshape(1, -1)

    cbf, sbf, clo, slo = _dft_table_call(seq)
    ztab = _hyena_pos_table(seq)
    max_decay = math.log(HY_DECAY_TARGET) / HY_FAST_DECAY
    min_decay = math.log(HY_DECAY_TARGET) / HY_SLOW_DECAY
    deltas = jnp.abs(jnp.linspace(min_decay, max_decay, HY_WIDTH, dtype=F32))[None, :]
    cos_t, sin_t = _rope_lane_tables(seq)
    w_in_r = _relayout_w_in(w_in)
    wq, wk, wvt = _relayout_mla(mla_w_uq, mla_w_ukv)
    wg_lru, bg_lru = _relayout_lru_gates(lru_wa, lru_wx, lru_ba, lru_bx)
    w1p = jnp.pad(hy_ffn_w1, ((0, 0), (0, LANE - HY_POS_EMB), (0, 0)))
    rw_p = jnp.pad(router_w, ((0, 0), (0, 0), (0, LANE - N_EXPERTS)))
    w_out_b = w_out.astype(BF16)

    ar, ai, any_ = _hyena_filter_call(ztab, w1p, hy_ffn_b1[:, None, :], hy_sin_f1[:, None, :], hy_ffn_w2,
                                      hy_ffn_b2[:, None, :], hy_sin_f2[:, None, :], hy_ffn_w3, deltas,
                                      cbf, sbf, clo, slo, seq)
    skip = hy_skip[:, :, None, :]

    xc = _ln_call(x.reshape(t, d), row(ln_in_g), row(ln_in_b), tm)
    for l in range(depth):
        proj = _proj_call(xc, w_in_r[l], tm).reshape(bsz, seq, PROJ_W)
        y_hy = _hyena_call(proj, hy_conv_w[l], row(hy_conv_b[l]), cbf, sbf, ar, ai, any_, skip, l)
        y_lr = _lru_call(proj, lru_conv_w[l], row(lru_conv_b[l]), wg_lru[l], bg_lru[l],
                         lru_lambda[l].reshape(1, -1))
        q, k, vt = _mla_prep_call(proj, row(mla_q_norm_g[l]), wq[l], row(mla_kv_norm_g[l]), wk[l], wvt[l],
                                  cos_t, sin_t)
        y_at = _attn_call(q, k, vt)
        x1, x1b = _postmix_call(y_hy.reshape(t, -1), y_lr.reshape(t, -1), y_at.reshape(t, -1), xc,
                                row(group_norm_g[l]), w_out_b[l], row(ln_mix_g[l]), row(ln_mix_b[l]), tm)
        x1 = x1.reshape(bsz, seq, d)
        pos, post, gatet = _select_call(x1, rw_p[l], cap)
        xe = _gather_call(pos, x1b.reshape(bsz, seq, d), cap)
        ye = _ffn_call(xe, exp_w_gate, exp_w_up, exp_w_down, l, cap)
        xc = _combine_call(post, gatet, ye, x1, row(ln_ffn_g[l]), row(ln_ffn_b[l]), cap).reshape(t, d)
    return xc.reshape(bsz, seq, d)
```

```python
import functools
import math

import jax
import jax.numpy as jnp
from jax import lax
from jax.experimental import pallas as pl
from jax.experimental.pallas import tpu as pltpu

F32 = jnp.float32
BF16 = jnp.bfloat16
HIGHEST = lax.Precision.HIGHEST

D_MODEL = 1024
HY_WIDTH = 256
LRU_WIDTH = 256
LRU_BLOCKS = 4
MLA_HEADS = 8
MLA_NOPE = 64
MLA_ROPE = 32
MLA_V = 64
MLA_Q_LORA = 768
MLA_KV_LORA = 256
ROPE_THETA = 10000.0
HY_POS_EMB = 33
HY_FILTER_FFN = 64
HY_FAST_DECAY = 0.3
HY_SLOW_DECAY = 1.5
HY_DECAY_TARGET = 1e-2
LRU_C = 8.0
N_EXPERTS = 16
EC_CAPACITY_FACTOR = 2
EPS = 1e-5
DEPTH = 4
ALPHA = (2.0 * DEPTH) ** 0.25

LANE = 128
MXU_N = 256
SUBLANES = 8
ROW_CHUNK = 256
HY_FREQ_BLOCK = 256
HY_LOOKAHEAD = 2
ATTN_COL_TILE = 256
ATTN_LOOKAHEAD = 4
HEAD_PAD = 128
PROJ_W = 2432
COL_HY, COL_CQ, COL_LR, COL_LG, COL_CKV, COL_KR = 0, 768, 1536, 1792, 2048, 2304
KR_LANE = 64
VMEM_CAP = 64 * 1024 * 1024


def _cparams(sem, vmem_mb):
    return pltpu.CompilerParams(dimension_semantics=sem, vmem_limit_bytes=vmem_mb * 1024 * 1024)


def _const_spec(shape):
    nd = len(shape)
    return pl.BlockSpec(shape, lambda *_: (0,) * nd, pipeline_mode=pl.Buffered(1))


def _layer_norm(x, g, b):
    xc = x - jnp.mean(x, axis=-1, keepdims=True)
    var = jnp.mean(xc * xc, axis=-1, keepdims=True)
    return xc * lax.rsqrt(var + EPS) * g + b


def _rms_norm(x, g):
    return x * lax.rsqrt(jnp.mean(x * x, axis=-1, keepdims=True) + EPS) * g


def _chunk_shift(load_rows, x, r0, d, seq):
    rc = x.shape[0]
    if d == 0:
        return x
    if 0 <= r0 - d and r0 - d + rc <= seq:
        return load_rows(r0 - d, rc)
    load_row = lambda r: load_rows(r, 1)
    t = lax.broadcasted_iota(jnp.int32, x.shape, 0)
    y = pltpu.roll(x, d % rc, axis=0)
    for k in range(abs(d)):
        dst = k if d > 0 else rc + d + k
        src = r0 - d + dst
        row = load_row(src) if 0 <= src < seq else jnp.zeros((1, x.shape[1]), x.dtype)
        y = jnp.where(t == dst, row, y)
    return y


def _sigmoid(x):
    return 1.0 / (1.0 + jnp.exp(-x))


def _gelu_tanh(x):
    c = math.sqrt(2.0 / math.pi)
    return 0.5 * x * (1.0 + jnp.tanh(c * (x + 0.044715 * (x * x * x))))


def _ln_kernel(x_ref, g_ref, b_ref, o_ref):
    o_ref[...] = _layer_norm(x_ref[...], g_ref[...], b_ref[...])


def _ln_call(x2d, g, b, tm):
    t, d = x2d.shape
    return pl.pallas_call(
        _ln_kernel,
        out_shape=jax.ShapeDtypeStruct((t, d), F32),
        grid=(t // tm,),
        in_specs=[pl.BlockSpec((tm, d), lambda i: (i, 0)), _const_spec((1, d)), _const_spec((1, d))],
        out_specs=pl.BlockSpec((tm, d), lambda i: (i, 0)),
        compiler_params=_cparams(("parallel",), 32),
        name="ln_in",
    )(x2d, g, b)


PROJ_COL_MAP = ((0, 768, COL_HY), (768, 256, COL_LR), (1024, 256, COL_LG), (1280, 768, COL_CQ),
                (2048, 256, COL_CKV))
PROJ_KR_SRC = 2304


def _proj_kernel(x_ref, w_ref, o_ref, w_scr):
    @pl.when(pl.program_id(0) == 0)
    def _():
        for src, width, dst in PROJ_COL_MAP:
            w_scr[:, dst:dst + width] = w_ref[0, :, src:src + width].astype(BF16)
        kr = w_ref[0, :, PROJ_KR_SRC:PROJ_KR_SRC + MLA_ROPE]
        rows = kr.shape[0]
        w_scr[:, COL_KR:COL_KR + LANE] = jnp.concatenate(
            [jnp.zeros((rows, KR_LANE), F32), kr, jnp.zeros((rows, LANE - KR_LANE - MLA_ROPE), F32)],
            axis=1).astype(BF16)

    o_ref[...] = jnp.dot(x_ref[...].astype(BF16), w_scr[...], preferred_element_type=F32)


def _proj_call(x2d, w_in, layer, tm):
    t, d = x2d.shape
    n_in = w_in.shape[2]
    return pl.pallas_call(
        _proj_kernel,
        out_shape=jax.ShapeDtypeStruct((t, PROJ_W), F32),
        grid=(t // tm,),
        in_specs=[pl.BlockSpec((tm, d), lambda i: (i, 0)),
                  pl.BlockSpec((1, d, n_in), lambda i: (layer, 0, 0), pipeline_mode=pl.Buffered(1))],
        out_specs=pl.BlockSpec((tm, PROJ_W), lambda i: (i, 0)),
        scratch_shapes=[pltpu.VMEM((d, PROJ_W), BF16)],
        compiler_params=_cparams(("arbitrary",), 48),
        name="proj",
    )(x2d, w_in)


def _split_bf16(x):
    hi = x.astype(BF16)
    return hi, (x - hi.astype(F32)).astype(BF16)


def _dft_table_kernel(chi_ref, shi_ref, clo_ref, slo_ref, tc_scr, ts_scr, *, seq, rb):
    j = pl.program_id(0)
    mask = 2 * seq - 1
    step = math.pi / seq

    @pl.when(j == 0)
    def _():
        fi = lax.broadcasted_iota(jnp.int32, (rb, seq), 0)
        t = lax.broadcasted_iota(jnp.int32, (rb, seq), 1)
        ang = ((fi * t) & mask).astype(F32) * step
        tc_scr[...] = jnp.cos(ang)
        ts_scr[...] = jnp.sin(ang)

    t = lax.broadcasted_iota(jnp.int32, (SUBLANES, seq), 1)
    ang = (((j * rb) * t) & mask).astype(F32) * step
    ca = jnp.cos(ang)[0:1, :]
    sa = jnp.sin(ang)[0:1, :]
    c = ca * tc_scr[...] - sa * ts_scr[...]
    s = sa * tc_scr[...] + ca * ts_scr[...]
    chi_ref[...], clo_ref[...] = _split_bf16(c)
    shi_ref[...], slo_ref[...] = _split_bf16(s)


def _dft_table_call(seq):
    assert seq & (seq - 1) == 0, "sequence length must be a power of two"
    rb = min(256, seq)
    blk = pl.BlockSpec((rb, seq), lambda j: (j, 0))
    return pl.pallas_call(
        functools.partial(_dft_table_kernel, seq=seq, rb=rb),
        out_shape=tuple(jax.ShapeDtypeStruct((seq, seq), BF16) for _ in range(4)),
        grid=(seq // rb,),
        in_specs=[],
        out_specs=(blk, blk, blk, blk),
        scratch_shapes=[pltpu.VMEM((rb, seq), F32), pltpu.VMEM((rb, seq), F32)],
        compiler_params=_cparams(("arbitrary",), 40),
        name="dft_tables",
    )()


def _hyena_filter_kernel(z_ref, w1_ref, b1_ref, f1_ref, w2_ref, b2_ref, f2_ref, w3_ref, dl_ref,
                         chi_ref, shi_ref, clo_ref, slo_ref, ar_ref, ai_ref, any_ref,
                         hs_hi, hs_lo, hd_hi, hd_lo, *, seq, fb):
    j = pl.program_id(1)
    n_fft = 2 * seq

    @pl.when(j == 0)
    def _():
        z = z_ref[...]
        h = jnp.sin(f1_ref[0] * (jnp.dot(z, w1_ref[0], precision=HIGHEST, preferred_element_type=F32)
                                 + b1_ref[0]))
        h = jnp.sin(f2_ref[0] * (jnp.dot(h, w2_ref[0], precision=HIGHEST, preferred_element_type=F32)
                                 + b2_ref[0]))
        t_idx = lax.broadcasted_iota(jnp.int32, (seq, HY_WIDTH), 0)
        t = t_idx.astype(F32) * (1.0 / (seq - 1))
        window = jnp.exp(-t * dl_ref[...])
        sign = jnp.where(t_idx % 2 == 0, 1.0, -1.0)

        def filt(g):
            w3g = w3_ref[0, :, g * HY_WIDTH:(g + 1) * HY_WIDTH]
            return jnp.dot(h, w3g, precision=HIGHEST, preferred_element_type=F32) * window

        for o in range(2):
            hf = filt(o)
            hb = filt(2 + o)
            hb0 = jnp.where(t_idx == 0, 0.0, hb)
            hs = hf + hb0
            cols = slice(o * HY_WIDTH, (o + 1) * HY_WIDTH)
            hs_hi[:, cols], hs_lo[:, cols] = _split_bf16(hs)
            hd_hi[:, cols], hd_lo[:, cols] = _split_bf16(hb0 - hf)
            any_ref[0, o] = jnp.sum(hs * sign, axis=0, keepdims=True) * (1.0 / n_fft)

    def dot3(t_hi, t_lo, h_hi, h_lo):
        return (jnp.dot(t_hi[...], h_hi[...], preferred_element_type=F32)
                + (jnp.dot(t_hi[...], h_lo[...], preferred_element_type=F32)
                   + jnp.dot(t_lo[...], h_hi[...], preferred_element_type=F32)))

    kr = dot3(chi_ref, clo_ref, hs_hi, hs_lo)
    ki = dot3(shi_ref, slo_ref, hd_hi, hd_lo)
    f_idx = lax.broadcasted_iota(jnp.int32, (fb, 2 * HY_WIDTH), 0) + j * fb
    wf = jnp.where(f_idx == 0, 1.0 / n_fft, 2.0 / n_fft)
    kr = kr * wf
    ki = ki * wf
    for o in range(2):
        ar_ref[0, o] = kr[:, o * HY_WIDTH:(o + 1) * HY_WIDTH]
        ai_ref[0, o] = ki[:, o * HY_WIDTH:(o + 1) * HY_WIDTH]


def _hyena_filter_call(ztab, w1, b1, f1, w2, b2, f2, w3, deltas, chi, shi, clo, slo, seq):
    depth = w1.shape[0]
    fb = min(512, seq)
    tab = pl.BlockSpec((fb, seq), lambda l, j: (j, 0))
    nf = seq // fb
    kern = functools.partial(_hyena_filter_kernel, seq=seq, fb=fb)
    lay = lambda l, j: (l, 0, 0)
    return pl.pallas_call(
        kern,
        out_shape=(jax.ShapeDtypeStruct((depth, 2, seq, HY_WIDTH), F32),
                   jax.ShapeDtypeStruct((depth, 2, seq, HY_WIDTH), F32),
                   jax.ShapeDtypeStruct((depth, 2, 1, HY_WIDTH), F32)),
        grid=(depth, nf),
        in_specs=[
            _const_spec(ztab.shape),
            pl.BlockSpec((1,) + w1.shape[1:], lay), pl.BlockSpec((1,) + b1.shape[1:], lay),
            pl.BlockSpec((1,) + f1.shape[1:], lay),
            pl.BlockSpec((1,) + w2.shape[1:], lay), pl.BlockSpec((1,) + b2.shape[1:], lay),
            pl.BlockSpec((1,) + f2.shape[1:], lay),
            pl.BlockSpec((1,) + w3.shape[1:], lay),
            _const_spec(deltas.shape), tab, tab, tab, tab,
        ],
        out_specs=(pl.BlockSpec((1, 2, fb, HY_WIDTH), lambda l, j: (l, 0, j, 0)),
                   pl.BlockSpec((1, 2, fb, HY_WIDTH), lambda l, j: (l, 0, j, 0)),
                   pl.BlockSpec((1, 2, 1, HY_WIDTH), lambda l, j: (l, 0, 0, 0))),
        scratch_shapes=[pltpu.VMEM((seq, 2 * HY_WIDTH), BF16) for _ in range(4)],
        compiler_params=_cparams(("arbitrary", "arbitrary"), 48),
        name="hyena_filter",
    )(ztab, w1, b1, f1, w2, b2, f2, w3, deltas, chi, shi, clo, slo)


def _hyena_kernel(p_ref, cw_ref, cb_ref, c_ref, s_ref, ar_ref, ai_ref, any_ref, sk_ref, o_ref,
                  g_scr, z_scr, zb_scr, acc_scr, *, seq, fb):
    w = HY_WIDTH
    rc = min(ROW_CHUNK, seq)
    load_row = lambda r, n: p_ref[0, r:r + n, :]
    for r0 in range(0, seq, rc):
        x = p_ref[0, r0:r0 + rc, :]
        u = (cw_ref[0:1, :] * _chunk_shift(load_row, x, r0, 1, seq) + cw_ref[1:2, :] * x
             + cw_ref[2:3, :] * _chunk_shift(load_row, x, r0, -1, seq) + cb_ref[...])
        g_scr[r0:r0 + rc, :] = u[:, 0:2 * w]
        z_scr[r0:r0 + rc, :] = u[:, 2 * w:3 * w]
    t_idx = lax.broadcasted_iota(jnp.int32, (seq, w), 0)
    sign = jnp.where(t_idx % 2 == 0, 1.0, -1.0)
    for o in range(2):
        z = z_scr[...]
        zb_scr[...] = z.astype(BF16)
        acc_scr[...] = sign * (jnp.sum(z * sign, axis=0, keepdims=True) * any_ref[0, o])
        nf = seq // fb

        def forward(j):
            rows = slice(j * fb, (j + 1) * fb)
            return (jnp.dot(c_ref[rows, :], zb_scr[...], preferred_element_type=F32),
                    jnp.dot(s_ref[rows, :], zb_scr[...], preferred_element_type=F32))

        pending = [forward(j) for j in range(min(HY_LOOKAHEAD, nf))]
        for j in range(nf):
            if j + HY_LOOKAHEAD < nf:
                pending.append(forward(j + HY_LOOKAHEAD))
            ur, us = pending.pop(0)
            rows = slice(j * fb, (j + 1) * fb)
            ar = ar_ref[0, o, rows, :]
            ai = ai_ref[0, o, rows, :]
            pp = (ur * ar + us * ai).astype(BF16)
            qq = (us * ar - ur * ai).astype(BF16)
            acc_scr[...] += (jnp.dot(c_ref[:, rows], pp, preferred_element_type=F32)
                             + jnp.dot(s_ref[:, rows], qq, preferred_element_type=F32))
        z_new = g_scr[:, o * w:(o + 1) * w] * (acc_scr[...] + sk_ref[0, o] * z_scr[...])
        if o == 0:
            z_scr[...] = z_new
        else:
            o_ref[0] = z_new.astype(o_ref.dtype)


def _hyena_call(proj, cw, cb, cbf, sbf, ar, ai, any_, skip, layer):
    bsz, seq, _ = proj.shape
    fb = min(HY_FREQ_BLOCK, seq)
    kern = functools.partial(_hyena_kernel, seq=seq, fb=fb)
    lsel = lambda b: (layer, 0, 0, 0)
    return pl.pallas_call(
        kern,
        out_shape=jax.ShapeDtypeStruct((bsz, seq, HY_WIDTH), BF16),
        grid=(bsz,),
        in_specs=[
            pl.BlockSpec((1, seq, 3 * HY_WIDTH), lambda b: (b, 0, COL_HY // (3 * HY_WIDTH)),
                         pipeline_mode=pl.Buffered(1)),
            _const_spec(cw.shape), _const_spec(cb.shape),
            _const_spec(cbf.shape), _const_spec(sbf.shape),
            pl.BlockSpec((1, 2, seq, HY_WIDTH), lsel, pipeline_mode=pl.Buffered(1)),
            pl.BlockSpec((1, 2, seq, HY_WIDTH), lsel, pipeline_mode=pl.Buffered(1)),
            pl.BlockSpec((1, 2, 1, HY_WIDTH), lsel, pipeline_mode=pl.Buffered(1)),
            pl.BlockSpec((1, 2, 1, HY_WIDTH), lsel, pipeline_mode=pl.Buffered(1)),
        ],
        out_specs=pl.BlockSpec((1, seq, HY_WIDTH), lambda b: (b, 0, 0)),
        scratch_shapes=[pltpu.VMEM((seq, 2 * HY_WIDTH), F32), pltpu.VMEM((seq, HY_WIDTH), F32),
                        pltpu.VMEM((seq, HY_WIDTH), BF16), pltpu.VMEM((seq, HY_WIDTH), F32)],
        compiler_params=_cparams(("parallel",), 60),
        name="hyena",
    )(proj, cw, cb, cbf, sbf, ar, ai, any_, skip)


def _lru_kernel(xr_ref, xg_ref, cw_ref, cb_ref, wg_ref, bg_ref, lam_ref, o_ref, a_scr, u_scr, *, seq):
    w = LRU_WIDTH
    rc = min(ROW_CHUNK, seq)
    lam = lam_ref[...]
    y = jnp.exp(-jnp.abs(lam))
    w1 = 1.0 + y
    log1p_y = jnp.where(w1 == 1.0, y, jnp.log(w1) * (y / jnp.where(w1 == 1.0, 1.0, w1 - 1.0)))
    softplus_neg = jnp.maximum(-lam, 0.0) + log1p_y
    load_row = lambda r, n: xr_ref[0, r:r + n, :]
    t_in = lax.broadcasted_iota(jnp.int32, (rc // SUBLANES, SUBLANES, w), 1)
    for r0 in range(0, seq, rc):
        x = xr_ref[0, r0:r0 + rc, :]
        xc = (cw_ref[0:1, :] * _chunk_shift(load_row, x, r0, 2, seq)
              + cw_ref[1:2, :] * _chunk_shift(load_row, x, r0, 1, seq)
              + cw_ref[2:3, :] * x + cw_ref[3:4, :] * _chunk_shift(load_row, x, r0, -1, seq) + cb_ref[...])
        xcb = xc.astype(BF16)
        for d in range(2):
            ca, cx = slice(d * w, (d + 1) * w), slice((2 + d) * w, (3 + d) * w)
            gate_a = _sigmoid(jnp.dot(xcb, wg_ref[:, ca], preferred_element_type=F32) + bg_ref[:, ca])
            gate_x = _sigmoid(jnp.dot(xcb, wg_ref[:, cx], preferred_element_type=F32) + bg_ref[:, cx])
            a = jnp.exp(-LRU_C * gate_a * softplus_neg[:, ca])
            u = jnp.sqrt(1.0 - a * a) * gate_x * xc
            a = a.reshape(rc // SUBLANES, SUBLANES, w)
            u = u.reshape(rc // SUBLANES, SUBLANES, w)
            s = 1
            while s < SUBLANES:
                keep = (t_in >= s) if d == 0 else (t_in < SUBLANES - s)
                sh = s if d == 0 else SUBLANES - s
                u = u + a * jnp.where(keep, pltpu.roll(u, sh, axis=1), 0.0)
                a = a * jnp.where(keep, pltpu.roll(a, sh, axis=1), 1.0)
                s *= 2
            a_scr[d, r0:r0 + rc, :] = a.reshape(rc, w)
            u_scr[d, r0:r0 + rc, :] = u.reshape(rc, w)

    ng = seq // SUBLANES

    def carry_step(i, carry):
        cf, cb = carry
        rf = pl.multiple_of(i * SUBLANES, SUBLANES)
        rb = pl.multiple_of((ng - 1 - i) * SUBLANES, SUBLANES)
        hf = u_scr[0, pl.ds(rf, SUBLANES), :] + a_scr[0, pl.ds(rf, SUBLANES), :] * cf
        hb = u_scr[1, pl.ds(rb, SUBLANES), :] + a_scr[1, pl.ds(rb, SUBLANES), :] * cb
        u_scr[0, pl.ds(rf, SUBLANES), :] = hf
        u_scr[1, pl.ds(rb, SUBLANES), :] = hb
        return hf[SUBLANES - 1:SUBLANES, :], hb[0:1, :]

    zero = jnp.zeros((1, w), F32)
    lax.fori_loop(0, ng, carry_step, (zero, zero), unroll=4)
    for r0 in range(0, seq, rc):
        rows = slice(r0, r0 + rc)
        o_ref[0, rows, :] = ((u_scr[0, rows, :] + u_scr[1, rows, :])
                             * _gelu_tanh(xg_ref[0, rows, :])).astype(o_ref.dtype)


def _lru_call(proj, cw, cb, wg, bg, lam):
    bsz, seq, _ = proj.shape
    w = LRU_WIDTH
    return pl.pallas_call(
        functools.partial(_lru_kernel, seq=seq),
        scratch_shapes=[pltpu.VMEM((2, seq, w), F32), pltpu.VMEM((2, seq, w), F32)],
        out_shape=jax.ShapeDtypeStruct((bsz, seq, w), BF16),
        grid=(bsz,),
        in_specs=[
            pl.BlockSpec((1, seq, w), lambda b: (b, 0, COL_LR // w)),
            pl.BlockSpec((1, seq, w), lambda b: (b, 0, COL_LG // w)),
            _const_spec(cw.shape), _const_spec(cb.shape), _const_spec(wg.shape),
            _const_spec(bg.shape), _const_spec(lam.shape),
        ],
        out_specs=pl.BlockSpec((1, seq, w), lambda b: (b, 0, 0)),
        compiler_params=_cparams(("parallel",), 48),
        name="rglru",
    )(proj, proj, cw, cb, wg, bg, lam)


def _rope128(x, cos, sin):
    lane = lax.broadcasted_iota(jnp.int32, x.shape, 1)
    half = MLA_ROPE // 2
    partner = jnp.where(lane < KR_LANE + half, pltpu.roll(x, LANE - half, axis=1),
                        pltpu.roll(x, half, axis=1))
    return x * cos + partner * sin


def _mla_prep_kernel(cq_ref, ckv_ref, kr_ref, qg_ref, wq_ref, kg_ref, wk_ref, wvt_ref, cos_ref,
                     sin_ref, q_ref, k_ref, vt_ref, *, qscale):
    cos = cos_ref[...]
    sin = sin_ref[...]
    qn = _rms_norm(cq_ref[0], qg_ref[...]).astype(BF16)
    q = jnp.dot(qn, wq_ref[...], preferred_element_type=F32)
    kvn = _rms_norm(ckv_ref[0], kg_ref[...]).astype(BF16)
    kn = jnp.dot(kvn, wk_ref[...], preferred_element_type=F32)
    v_t = lax.dot_general(wvt_ref[...], kvn, (((1,), (1,)), ((), ())), preferred_element_type=F32)
    row = lax.broadcasted_iota(jnp.int32, v_t.shape, 0)
    vt_ref[0] = jnp.where((row & (HEAD_PAD - 1)) == MLA_V, 1.0, v_t).astype(BF16)
    kr = _rope128(kr_ref[0], cos, sin)
    for h in range(MLA_HEADS):
        cols = slice(h * HEAD_PAD, (h + 1) * HEAD_PAD)
        q_ref[0, :, cols] = (_rope128(q[:, cols], cos, sin) * qscale).astype(BF16)
        k_ref[0, :, cols] = (kn[:, cols] + kr).astype(BF16)


def _mla_prep_call(proj, qg, wq, kg, wk, wvt, cos_t, sin_t):
    bsz, seq, _ = proj.shape
    hp = MLA_HEADS * HEAD_PAD
    tr = min(512, seq)
    qscale = float((MLA_NOPE + MLA_ROPE) ** -0.5 * math.log2(math.e))
    return pl.pallas_call(
        functools.partial(_mla_prep_kernel, qscale=qscale),
        out_shape=(jax.ShapeDtypeStruct((bsz, seq, hp), BF16),
                   jax.ShapeDtypeStruct((bsz, seq, hp), BF16),
                   jax.ShapeDtypeStruct((bsz, hp, seq), BF16)),
        grid=(bsz, seq // tr),
        in_specs=[
            pl.BlockSpec((1, tr, MLA_Q_LORA), lambda b, i: (b, i, COL_CQ // MLA_Q_LORA)),
            pl.BlockSpec((1, tr, MLA_KV_LORA), lambda b, i: (b, i, COL_CKV // MLA_KV_LORA)),
            pl.BlockSpec((1, tr, LANE), lambda b, i: (b, i, COL_KR // LANE)),
            _const_spec(qg.shape), _const_spec(wq.shape), _const_spec(kg.shape),
            _const_spec(wk.shape), _const_spec(wvt.shape),
            pl.BlockSpec((tr, LANE), lambda b, i: (i, 0)), pl.BlockSpec((tr, LANE), lambda b, i: (i, 0)),
        ],
        out_specs=(pl.BlockSpec((1, tr, hp), lambda b, i: (b, i, 0)),
                   pl.BlockSpec((1, tr, hp), lambda b, i: (b, i, 0)),
                   pl.BlockSpec((1, hp, tr), lambda b, i: (b, 0, i))),
        compiler_params=_cparams(("parallel", "parallel"), 48),
        name="mla_prep",
    )(proj, proj, proj, qg, wq, kg, wk, wvt, cos_t, sin_t)


def _attn_kernel(q_ref, k_ref, vt_ref, o_ref, s_scr, p_scr):
    tq = q_ref.shape[1]
    ct = min(ATTN_COL_TILE, tq)
    chains = [(h, c0) for c0 in range(0, tq, ct) for h in range(2)]
    nt = (((1,), (1,)), ((), ()))

    ns = s_scr.shape[0]

    def scores(i):
        h, c0 = chains[i]
        cols = slice(h * HEAD_PAD, (h + 1) * HEAD_PAD)
        s_scr[i % ns] = lax.dot_general(k_ref[0, :, cols], q_ref[0, c0:c0 + ct, cols], nt,
                                        preferred_element_type=F32)

    def finish(i):
        h = chains[i][0]
        s_t = s_scr[i % ns]
        p_scr[i % 2] = jnp.exp2(s_t - jnp.max(s_t, axis=0, keepdims=True)).astype(BF16)
        o_t = jnp.dot(vt_ref[0, h * HEAD_PAD:(h + 1) * HEAD_PAD, :], p_scr[i % 2],
                      preferred_element_type=F32)
        return o_t[0:MLA_V, :] / o_t[MLA_V:MLA_V + 1, :]

    for i in range(min(ATTN_LOOKAHEAD, len(chains))):
        scores(i)
    done = {}
    for i, (h, c0) in enumerate(chains):
        if i + ATTN_LOOKAHEAD < len(chains):
            scores(i + ATTN_LOOKAHEAD)
        done[h] = finish(i)
        if h == 1:
            o_ref[0, c0:c0 + ct, :] = jnp.concatenate([done[0], done[1]], axis=0).T.astype(o_ref.dtype)


def _attn_call(q, k, vt):
    bsz, seq, hp = q.shape
    tq = min(2048, seq)
    pw = 2 * HEAD_PAD
    return pl.pallas_call(
        _attn_kernel,
        out_shape=jax.ShapeDtypeStruct((bsz, seq, MLA_HEADS * MLA_V), BF16),
        grid=(bsz, MLA_HEADS // 2, seq // tq),
        in_specs=[pl.BlockSpec((1, tq, pw), lambda b, j, i: (b, i, j)),
                  pl.BlockSpec((1, seq, pw), lambda b, j, i: (b, 0, j)),
                  pl.BlockSpec((1, pw, seq), lambda b, j, i: (b, j, 0))],
        out_specs=pl.BlockSpec((1, tq, HEAD_PAD), lambda b, j, i: (b, i, j)),
        scratch_shapes=[pltpu.VMEM((ATTN_LOOKAHEAD + 1, seq, min(ATTN_COL_TILE, tq)), F32),
                        pltpu.VMEM((2, seq, min(ATTN_COL_TILE, tq)), BF16)],
        compiler_params=_cparams(("parallel", "parallel", "parallel"), 48),
        name="mla_attn",
    )(q, k, vt)


def _postmix_kernel(yh_ref, yl_ref, ya_ref, x_ref, gn_ref, wo_ref, g_ref, b_ref, o_ref, ob_ref):
    g0, g1 = HY_WIDTH, HY_WIDTH + LRU_WIDTH
    y = jnp.concatenate([_rms_norm(yh_ref[...].astype(F32), gn_ref[:, 0:g0]),
                         _rms_norm(yl_ref[...].astype(F32), gn_ref[:, g0:g1]),
                         _rms_norm(ya_ref[...].astype(F32), gn_ref[:, g1:])], axis=-1).astype(BF16)
    mix = jnp.dot(y, wo_ref[...], preferred_element_type=F32)
    x1 = _layer_norm(ALPHA * x_ref[...] + mix, g_ref[...], b_ref[...])
    o_ref[...] = x1
    ob_ref[...] = x1.astype(BF16)


def _postmix_call(yh, yl, ya, x2d, gn, wo, g, b, tm):
    t, d = x2d.shape
    row = lambda w: pl.BlockSpec((tm, w), lambda i: (i, 0))
    return pl.pallas_call(
        _postmix_kernel,
        out_shape=(jax.ShapeDtypeStruct((t, d), F32), jax.ShapeDtypeStruct((t, d), BF16)),
        grid=(t // tm,),
        in_specs=[row(HY_WIDTH), row(LRU_WIDTH), row(MLA_HEADS * MLA_V), row(d),
                  _const_spec(gn.shape), _const_spec(wo.shape), _const_spec(g.shape), _const_spec(b.shape)],
        out_specs=(row(d), row(d)),
        compiler_params=_cparams(("parallel",), 48),
        name="postmix",
    )(yh, yl, ya, x2d, gn, wo, g, b)


def _cumsum_lanes(x):
    n = x.shape[1]
    lane = lax.broadcasted_iota(jnp.int32, x.shape, 1)
    d = 1
    while d < n:
        x = x + jnp.where(lane >= d, pltpu.roll(x, d, axis=1), 0.0)
        d *= 2
    return x


def _select_kernel(x_ref, rw_ref, pos_ref, post_ref, gatet_ref, *, cap):
    x = x_ref[0]
    x_hi, x_lo = _split_bf16(x)
    w_hi, w_lo = _split_bf16(rw_ref[...])
    logits = (jnp.dot(x_hi, w_hi, preferred_element_type=F32)
              + (jnp.dot(x_hi, w_lo, preferred_element_type=F32)
                 + jnp.dot(x_lo, w_hi, preferred_element_type=F32)))
    lane = lax.broadcasted_iota(jnp.int32, logits.shape, 1)
    valid = lane < N_EXPERTS
    logits = jnp.where(valid, logits, -1e30)
    m = jnp.max(logits, axis=-1, keepdims=True)
    ex = jnp.where(valid, jnp.exp(logits - m), 0.0)
    aff = ex / jnp.sum(ex, axis=-1, keepdims=True)
    seq = aff.shape[0]
    aff_e = aff.T[0:N_EXPERTS, :]

    def body(_, carry):
        lo, hi = carry
        mid = lo + ((hi - lo + 1) >> 1)
        cnt = jnp.sum(jnp.where(aff_e >= pltpu.bitcast(mid, F32), 1.0, 0.0), axis=1, keepdims=True)
        ok = cnt >= float(cap)
        return jnp.where(ok, mid, lo), jnp.where(ok, hi, mid - 1)

    lo0 = jnp.zeros((N_EXPERTS, 1), jnp.int32)
    hi0 = jnp.full((N_EXPERTS, 1), 0x7F7FFFFF, jnp.int32)
    thr_bits, _ = lax.fori_loop(0, 31, body, (lo0, hi0))
    thr = pltpu.bitcast(thr_bits, F32)
    gt = aff_e > thr
    eq = aff_e == thr
    need = float(cap) - jnp.sum(jnp.where(gt, 1.0, 0.0), axis=1, keepdims=True)
    eq_rank = _cumsum_lanes(jnp.where(eq, 1.0, 0.0))
    sel = gt | (eq & (eq_rank <= need))
    pos = jnp.where(sel, _cumsum_lanes(jnp.where(sel, 1.0, 0.0)) - 1.0, -1.0)
    pos_ref[0] = pos
    fill = jnp.full((LANE - N_EXPERTS, seq), -1.0, F32)
    post_ref[0] = jnp.concatenate([pos, fill], axis=0).T
    gatet_ref[0] = jnp.concatenate([jnp.where(sel, aff_e, 0.0), fill], axis=0).T


def _select_call(x1, rw, cap):
    bsz, seq, d = x1.shape
    kern = functools.partial(_select_kernel, cap=cap)
    return pl.pallas_call(
        kern,
        out_shape=(jax.ShapeDtypeStruct((bsz, N_EXPERTS, seq), F32),
                   jax.ShapeDtypeStruct((bsz, seq, LANE), F32),
                   jax.ShapeDtypeStruct((bsz, seq, LANE), F32)),
        grid=(bsz,),
        in_specs=[pl.BlockSpec((1, seq, d), lambda b: (b, 0, 0)), _const_spec(rw.shape)],
        out_specs=(pl.BlockSpec((1, N_EXPERTS, seq), lambda b: (b, 0, 0)),
                   pl.BlockSpec((1, seq, LANE), lambda b: (b, 0, 0)),
                   pl.BlockSpec((1, seq, LANE), lambda b: (b, 0, 0))),
        compiler_params=_cparams(("parallel",), 48),
        name="moe_select",
    )(x1, rw)


def _gather_kernel(pos_ref, x_ref, o_ref, p_scr, *, cap):
    seq, d = x_ref.shape[1], x_ref.shape[2]
    slot = lax.broadcasted_iota(jnp.int32, (cap, seq), 0).astype(F32)
    for e in range(N_EXPERTS):
        p_scr[e * cap:(e + 1) * cap, :] = jnp.where(pos_ref[0, e:e + 1, :] == slot, 1.0, 0.0).astype(BF16)
    cw = min(MXU_N, d)
    for c in range(0, d, cw):
        o_ref[0, :, c:c + cw] = jnp.dot(p_scr[...], x_ref[0, :, c:c + cw],
                                        preferred_element_type=F32).astype(BF16)


def _gather_call(pos, x1b, cap):
    bsz, seq, d = x1b.shape
    return pl.pallas_call(
        functools.partial(_gather_kernel, cap=cap),
        out_shape=jax.ShapeDtypeStruct((bsz, N_EXPERTS * cap, d), BF16),
        grid=(bsz,),
        in_specs=[pl.BlockSpec((1, N_EXPERTS, seq), lambda b: (b, 0, 0)),
                  pl.BlockSpec((1, seq, d), lambda b: (b, 0, 0))],
        out_specs=pl.BlockSpec((1, N_EXPERTS * cap, d), lambda b: (b, 0, 0)),
        scratch_shapes=[pltpu.VMEM((N_EXPERTS * cap, seq), BF16)],
        compiler_params=_cparams(("parallel",), 56),
        name="moe_gather",
    )(pos, x1b)


def _ffn_kernel(xe_ref, wg_ref, wu_ref, wd_ref, o_ref, acc_scr, *, rows_per_dot):
    f = pl.program_id(1)
    bsz, cap, d = xe_ref.shape
    wg = wg_ref[0, 0].astype(BF16)
    wu = wu_ref[0, 0].astype(BF16)
    wd = wd_ref[0, 0].astype(BF16)
    nb = rows_per_dot // cap

    @pl.when(f == 0)
    def _():
        acc_scr[...] = jnp.zeros_like(acc_scr)

    def up(b0):
        xe = xe_ref[b0:b0 + nb].reshape(nb * cap, d)
        return (jnp.dot(xe, wg, preferred_element_type=F32), jnp.dot(xe, wu, preferred_element_type=F32))

    starts = list(range(0, bsz, nb))
    nxt = up(starts[0])
    for i, b0 in enumerate(starts):
        hg, hu = nxt
        if i + 1 < len(starts):
            nxt = up(starts[i + 1])
        hid = (hg * _sigmoid(hg) * hu).astype(BF16)
        rows = slice(b0 * cap, (b0 + nb) * cap)
        acc_scr[rows, :] += jnp.dot(hid, wd, preferred_element_type=F32)

    @pl.when(f == pl.num_programs(1) - 1)
    def _():
        o_ref[...] = acc_scr[...].reshape(bsz, cap, d).astype(BF16)


def _ffn_call(xe, wg, wu, wd, layer, cap):
    bsz, _, d = xe.shape
    _, ne, _, ff = wg.shape
    fcw = min(512, ff)
    rows_per_dot = cap * max(1, min(bsz, 1024 // cap))
    return pl.pallas_call(
        functools.partial(_ffn_kernel, rows_per_dot=rows_per_dot),
        out_shape=jax.ShapeDtypeStruct((bsz, ne * cap, d), BF16),
        grid=(ne, ff // fcw),
        in_specs=[pl.BlockSpec((bsz, cap, d), lambda e, f: (0, e, 0)),
                  pl.BlockSpec((1, 1, d, fcw), lambda e, f: (layer, e, 0, f)),
                  pl.BlockSpec((1, 1, d, fcw), lambda e, f: (layer, e, 0, f)),
                  pl.BlockSpec((1, 1, fcw, d), lambda e, f: (layer, e, f, 0))],
        out_specs=pl.BlockSpec((bsz, cap, d), lambda e, f: (0, e, 0)),
        scratch_shapes=[pltpu.VMEM((bsz * cap, d), F32)],
        compiler_params=_cparams(("parallel", "arbitrary"), 56),
        name="moe_ffn",
    )(xe, wg, wu, wd)


def _combine_kernel(post_ref, gatet_ref, ye_ref, x_ref, g_ref, b_ref, o_ref, *, cap):
    tr = x_ref.shape[1]
    slot = lax.broadcasted_iota(jnp.int32, (tr, cap), 1).astype(F32)
    post = post_ref[0]
    gatet = gatet_ref[0]
    acc = ALPHA * x_ref[0]
    for e in range(N_EXPERTS):
        scat = jnp.where(post[:, e:e + 1] == slot, gatet[:, e:e + 1], 0.0).astype(BF16)
        acc = acc + jnp.dot(scat, ye_ref[0, e * cap:(e + 1) * cap, :], preferred_element_type=F32)
    o_ref[0] = _layer_norm(acc, g_ref[...], b_ref[...])


def _combine_call(post, gatet, ye, x1, g, b, cap):
    bsz, seq, d = x1.shape
    tr = min(512, seq)
    kern = functools.partial(_combine_kernel, cap=cap)
    return pl.pallas_call(
        kern,
        out_shape=jax.ShapeDtypeStruct((bsz, seq, d), F32),
        grid=(bsz, seq // tr),
        in_specs=[pl.BlockSpec((1, tr, LANE), lambda bi, i: (bi, i, 0)),
                  pl.BlockSpec((1, tr, LANE), lambda bi, i: (bi, i, 0)),
                  pl.BlockSpec((1, N_EXPERTS * cap, d), lambda bi, i: (bi, 0, 0)),
                  pl.BlockSpec((1, tr, d), lambda bi, i: (bi, i, 0)),
                  _const_spec(g.shape), _const_spec(b.shape)],
        out_specs=pl.BlockSpec((1, tr, d), lambda bi, i: (bi, i, 0)),
        compiler_params=_cparams(("parallel", "parallel"), 48),
        name="moe_combine",
    )(post, gatet, ye, x1, g, b)


def _hyena_pos_table(seq):
    t = jnp.linspace(0.0, 1.0, seq, dtype=F32)[:, None]
    bands = (HY_POS_EMB - 1) // 2
    t_idx = jnp.arange(seq, dtype=F32)[:, None]
    freqs = jnp.linspace(1e-4, bands - 1, bands, dtype=F32)[None, :]
    ang = 2.0 * math.pi * t_idx * freqs / seq
    z = jnp.concatenate([t, jnp.cos(ang), -jnp.sin(ang)], axis=-1)
    return jnp.pad(z, ((0, 0), (0, LANE - HY_POS_EMB)))


def _rope_lane_tables(seq):
    half = MLA_ROPE // 2
    inv = ROPE_THETA ** (-jnp.arange(0, MLA_ROPE, 2, dtype=F32) / MLA_ROPE)
    ang = jnp.arange(seq, dtype=F32)[:, None] * inv[None, :]
    cos, sin = jnp.cos(ang), jnp.sin(ang)
    ones = jnp.ones((seq, KR_LANE), F32)
    zeros = jnp.zeros((seq, KR_LANE), F32)
    tail = LANE - KR_LANE - MLA_ROPE
    cos_t = jnp.concatenate([ones, cos, cos, jnp.ones((seq, tail), F32)], axis=-1)
    sin_t = jnp.concatenate([zeros, -sin, sin, jnp.zeros((seq, tail), F32)], axis=-1)
    return cos_t, sin_t


def _relayout_mla(w_uq, w_ukv):
    depth = w_uq.shape[0]
    qd = MLA_NOPE + MLA_ROPE
    wq = w_uq.reshape(depth, MLA_Q_LORA, MLA_HEADS, qd)
    wq = jnp.pad(wq, ((0, 0), (0, 0), (0, 0), (0, HEAD_PAD - qd)))
    wq = wq.reshape(depth, MLA_Q_LORA, MLA_HEADS * HEAD_PAD).astype(BF16)
    wkv = w_ukv.reshape(depth, MLA_KV_LORA, MLA_HEADS, MLA_NOPE + MLA_V)
    wk = jnp.pad(wkv[..., :MLA_NOPE], ((0, 0), (0, 0), (0, 0), (0, HEAD_PAD - MLA_NOPE)))
    wk = wk.reshape(depth, MLA_KV_LORA, MLA_HEADS * HEAD_PAD).astype(BF16)
    wv = jnp.pad(wkv[..., MLA_NOPE:], ((0, 0), (0, 0), (0, 0), (0, HEAD_PAD - MLA_V)))
    wvt = jnp.swapaxes(wv.reshape(depth, MLA_KV_LORA, MLA_HEADS * HEAD_PAD), 1, 2).astype(BF16)
    return wq, wk, wvt


def _relayout_lru_gates(wa, wx, ba, bx):
    depth = wa.shape[0]
    bw = LRU_WIDTH // LRU_BLOCKS
    eye = jnp.eye(LRU_BLOCKS, dtype=wa.dtype)

    def bd(w):
        full = jnp.einsum("ldnjk,nm->ldnjmk", w, eye)
        return full.reshape(depth, 2, LRU_WIDTH, LRU_WIDTH)

    a, x = bd(wa), bd(wx)
    wg = jnp.concatenate([a[:, 0], a[:, 1], x[:, 0], x[:, 1]], axis=-1).astype(BF16)
    bg = jnp.concatenate([ba[:, 0], ba[:, 1], bx[:, 0], bx[:, 1]], axis=-1)[:, None, :]
    del bw
    return wg, bg


def kernel(x, ln_in_g, ln_in_b, w_in, hy_conv_w, hy_conv_b, hy_ffn_w1, hy_ffn_b1, hy_sin_f1,
           hy_ffn_w2, hy_ffn_b2, hy_sin_f2, hy_ffn_w3, hy_skip, lru_conv_w, lru_conv_b, lru_wa, lru_ba,
           lru_wx, lru_bx, lru_lambda, mla_q_norm_g, mla_w_uq, mla_kv_norm_g, mla_w_ukv, group_norm_g,
           w_out, ln_mix_g, ln_mix_b, router_w, exp_w_gate, exp_w_up, exp_w_down, ln_ffn_g, ln_ffn_b):
    bsz, seq, d = x.shape
    depth = w_in.shape[0]
    t = bsz * seq
    tm = min(512, t)
    cap = max(1, EC_CAPACITY_FACTOR * seq // N_EXPERTS)
    row = lambda v: v.reshape(1, -1)

    cbf, sbf, clo, slo = _dft_table_call(seq)
    ztab = _hyena_pos_table(seq)
    max_decay = math.log(HY_DECAY_TARGET) / HY_FAST_DECAY
    min_decay = math.log(HY_DECAY_TARGET) / HY_SLOW_DECAY
    deltas = jnp.abs(jnp.linspace(min_decay, max_decay, HY_WIDTH, dtype=F32))[None, :]
    cos_t, sin_t = _rope_lane_tables(seq)
    wq, wk, wvt = _relayout_mla(mla_w_uq, mla_w_ukv)
    wg_lru, bg_lru = _relayout_lru_gates(lru_wa, lru_wx, lru_ba, lru_bx)
    w1p = jnp.pad(hy_ffn_w1, ((0, 0), (0, LANE - HY_POS_EMB), (0, 0)))
    rw_p = jnp.pad(router_w, ((0, 0), (0, 0), (0, LANE - N_EXPERTS)))
    w_out_b = w_out.astype(BF16)

    ar, ai, any_ = _hyena_filter_call(ztab, w1p, hy_ffn_b1[:, None, :], hy_sin_f1[:, None, :], hy_ffn_w2,
                                      hy_ffn_b2[:, None, :], hy_sin_f2[:, None, :], hy_ffn_w3, deltas,
                                      cbf, sbf, clo, slo, seq)
    skip = hy_skip[:, :, None, :]

    xc = _ln_call(x.reshape(t, d), row(ln_in_g), row(ln_in_b), tm)
    for l in range(depth):
        proj = _proj_call(xc, w_in, l, tm).reshape(bsz, seq, PROJ_W)
        y_hy = _hyena_call(proj, hy_conv_w[l], row(hy_conv_b[l]), cbf, sbf, ar, ai, any_, skip, l)
        y_lr = _lru_call(proj, lru_conv_w[l], row(lru_conv_b[l]), wg_lru[l], bg_lru[l],
                         lru_lambda[l].reshape(1, -1))
        q, k, vt = _mla_prep_call(proj, row(mla_q_norm_g[l]), wq[l], row(mla_kv_norm_g[l]), wk[l], wvt[l],
                                  cos_t, sin_t)
        y_at = _attn_call(q, k, vt)
        x1, x1b = _postmix_call(y_hy.reshape(t, -1), y_lr.reshape(t, -1), y_at.reshape(t, -1), xc,
                                row(group_norm_g[l]), w_out_b[l], row(ln_mix_g[l]), row(ln_mix_b[l]), tm)
        x1 = x1.reshape(bsz, seq, d)
        pos, post, gatet = _select_call(x1, rw_p[l], cap)
        xe = _gather_call(pos, x1b.reshape(bsz, seq, d), cap)
        ye = _ffn_call(xe, exp_w_gate, exp_w_up, exp_w_down, l, cap)
        xc = _combine_call(post, gatet, ye, x1, row(ln_ffn_g[l]), row(ln_ffn_b[l]), cap).reshape(t, d)
    return xc.reshape(bsz, seq, d)
```

```python
import functools
import math

import jax
import jax.numpy as jnp
from jax import lax
from jax.experimental import pallas as pl
from jax.experimental.pallas import tpu as pltpu

F32 = jnp.float32
BF16 = jnp.bfloat16
HIGHEST = lax.Precision.HIGHEST

D_MODEL = 1024
HY_WIDTH = 256
LRU_WIDTH = 256
LRU_BLOCKS = 4
MLA_HEADS = 8
MLA_NOPE = 64
MLA_ROPE = 32
MLA_V = 64
MLA_Q_LORA = 768
MLA_KV_LORA = 256
ROPE_THETA = 10000.0
HY_POS_EMB = 33
HY_FILTER_FFN = 64
HY_FAST_DECAY = 0.3
HY_SLOW_DECAY = 1.5
HY_DECAY_TARGET = 1e-2
LRU_C = 8.0
N_EXPERTS = 16
EC_CAPACITY_FACTOR = 2
EPS = 1e-5
DEPTH = 4
ALPHA = (2.0 * DEPTH) ** 0.25

LANE = 128
MXU_N = 256
SUBLANES = 8
ROW_CHUNK = 256
HY_FREQ_BLOCK = 256
HY_LOOKAHEAD = 2
ATTN_COL_TILE = 256
ATTN_LOOKAHEAD = 4
HEAD_PAD = 128
PROJ_W = 2432
COL_HY, COL_CQ, COL_LR, COL_LG, COL_CKV, COL_KR = 0, 768, 1536, 1792, 2048, 2304
KR_LANE = 64
VMEM_CAP = 64 * 1024 * 1024


def _cparams(sem, vmem_mb):
    return pltpu.CompilerParams(dimension_semantics=sem, vmem_limit_bytes=vmem_mb * 1024 * 1024)


def _const_spec(shape):
    nd = len(shape)
    return pl.BlockSpec(shape, lambda *_: (0,) * nd, pipeline_mode=pl.Buffered(1))


def _layer_norm(x, g, b):
    xc = x - jnp.mean(x, axis=-1, keepdims=True)
    var = jnp.mean(xc * xc, axis=-1, keepdims=True)
    return xc * lax.rsqrt(var + EPS) * g + b


def _rms_norm(x, g):
    return x * lax.rsqrt(jnp.mean(x * x, axis=-1, keepdims=True) + EPS) * g


def _chunk_shift(load_rows, x, r0, d, seq):
    rc = x.shape[0]
    if d == 0:
        return x
    if 0 <= r0 - d and r0 - d + rc <= seq:
        return load_rows(r0 - d, rc)
    load_row = lambda r: load_rows(r, 1)
    t = lax.broadcasted_iota(jnp.int32, x.shape, 0)
    y = pltpu.roll(x, d % rc, axis=0)
    for k in range(abs(d)):
        dst = k if d > 0 else rc + d + k
        src = r0 - d + dst
        row = load_row(src) if 0 <= src < seq else jnp.zeros((1, x.shape[1]), x.dtype)
        y = jnp.where(t == dst, row, y)
    return y


def _sigmoid(x):
    return 1.0 / (1.0 + jnp.exp(-x))


def _gelu_tanh(x):
    c = math.sqrt(2.0 / math.pi)
    return 0.5 * x * (1.0 + jnp.tanh(c * (x + 0.044715 * (x * x * x))))


def _ln_kernel(x_ref, g_ref, b_ref, o_ref):
    o_ref[...] = _layer_norm(x_ref[...], g_ref[...], b_ref[...])


def _ln_call(x2d, g, b, tm):
    t, d = x2d.shape
    return pl.pallas_call(
        _ln_kernel,
        out_shape=jax.ShapeDtypeStruct((t, d), F32),
        grid=(t // tm,),
        in_specs=[pl.BlockSpec((tm, d), lambda i: (i, 0)), _const_spec((1, d)), _const_spec((1, d))],
        out_specs=pl.BlockSpec((tm, d), lambda i: (i, 0)),
        compiler_params=_cparams(("parallel",), 32),
        name="ln_in",
    )(x2d, g, b)


PROJ_COL_MAP = ((0, 768, COL_HY), (768, 256, COL_LR), (1024, 256, COL_LG), (1280, 768, COL_CQ),
                (2048, 256, COL_CKV))
PROJ_KR_SRC = 2304


def _proj_kernel(x_ref, w_ref, o_ref, w_scr):
    @pl.when(pl.program_id(0) == 0)
    def _():
        for src, width, dst in PROJ_COL_MAP:
            w_scr[:, dst:dst + width] = w_ref[0, :, src:src + width].astype(BF16)
        kr = w_ref[0, :, PROJ_KR_SRC:PROJ_KR_SRC + MLA_ROPE]
        rows = kr.shape[0]
        w_scr[:, COL_KR:COL_KR + LANE] = jnp.concatenate(
            [jnp.zeros((rows, KR_LANE), F32), kr, jnp.zeros((rows, LANE - KR_LANE - MLA_ROPE), F32)],
            axis=1).astype(BF16)

    o_ref[...] = jnp.dot(x_ref[...].astype(BF16), w_scr[...], preferred_element_type=F32)


def _proj_call(x2d, w_in, layer, tm):
    t, d = x2d.shape
    n_in = w_in.shape[2]
    return pl.pallas_call(
        _proj_kernel,
        out_shape=jax.ShapeDtypeStruct((t, PROJ_W), F32),
        grid=(t // tm,),
        in_specs=[pl.BlockSpec((tm, d), lambda i: (i, 0)),
                  pl.BlockSpec((1, d, n_in), lambda i: (layer, 0, 0), pipeline_mode=pl.Buffered(1))],
        out_specs=pl.BlockSpec((tm, PROJ_W), lambda i: (i, 0)),
        scratch_shapes=[pltpu.VMEM((d, PROJ_W), BF16)],
        compiler_params=_cparams(("arbitrary",), 48),
        name="proj",
    )(x2d, w_in)


def _split_bf16(x):
    hi = x.astype(BF16)
    return hi, (x - hi.astype(F32)).astype(BF16)


def _dft_table_kernel(c_ref, s_ref, tc_scr, ts_scr, *, seq, rb):
    j = pl.program_id(0)
    mask = 2 * seq - 1
    step = math.pi / seq

    @pl.when(j == 0)
    def _():
        fi = lax.broadcasted_iota(jnp.int32, (rb, seq), 0)
        t = lax.broadcasted_iota(jnp.int32, (rb, seq), 1)
        ang = ((fi * t) & mask).astype(F32) * step
        tc_scr[...] = jnp.cos(ang)
        ts_scr[...] = jnp.sin(ang)

    t = lax.broadcasted_iota(jnp.int32, (SUBLANES, seq), 1)
    ang = (((j * rb) * t) & mask).astype(F32) * step
    ca = jnp.cos(ang)[0:1, :]
    sa = jnp.sin(ang)[0:1, :]
    c = ca * tc_scr[...] - sa * ts_scr[...]
    s = sa * tc_scr[...] + ca * ts_scr[...]
    c_ref[...] = c.astype(BF16)
    s_ref[...] = s.astype(BF16)


def _dft_table_call(seq):
    assert seq & (seq - 1) == 0, "sequence length must be a power of two"
    rb = min(256, seq)
    blk = pl.BlockSpec((rb, seq), lambda j: (j, 0))
    return pl.pallas_call(
        functools.partial(_dft_table_kernel, seq=seq, rb=rb),
        out_shape=tuple(jax.ShapeDtypeStruct((seq, seq), BF16) for _ in range(2)),
        grid=(seq // rb,),
        in_specs=[],
        out_specs=(blk, blk),
        scratch_shapes=[pltpu.VMEM((rb, seq), F32), pltpu.VMEM((rb, seq), F32)],
        compiler_params=_cparams(("arbitrary",), 40),
        name="dft_tables",
    )()


def _hyena_filter_kernel(z_ref, w1_ref, b1_ref, f1_ref, w2_ref, b2_ref, f2_ref, w3_ref, dl_ref,
                         c_ref, s_ref, ar_ref, ai_ref, any_ref,
                         hs_hi, hs_lo, hd_hi, hd_lo, *, seq, fb):
    j = pl.program_id(1)
    n_fft = 2 * seq

    @pl.when(j == 0)
    def _():
        z = z_ref[...]
        h = jnp.sin(f1_ref[0] * (jnp.dot(z, w1_ref[0], precision=HIGHEST, preferred_element_type=F32)
                                 + b1_ref[0]))
        h = jnp.sin(f2_ref[0] * (jnp.dot(h, w2_ref[0], precision=HIGHEST, preferred_element_type=F32)
                                 + b2_ref[0]))
        t_idx = lax.broadcasted_iota(jnp.int32, (seq, HY_WIDTH), 0)
        t = t_idx.astype(F32) * (1.0 / (seq - 1))
        window = jnp.exp(-t * dl_ref[...])
        sign = jnp.where(t_idx % 2 == 0, 1.0, -1.0)

        def filt(g):
            w3g = w3_ref[0, :, g * HY_WIDTH:(g + 1) * HY_WIDTH]
            return jnp.dot(h, w3g, precision=HIGHEST, preferred_element_type=F32) * window

        for o in range(2):
            hf = filt(o)
            hb = filt(2 + o)
            hb0 = jnp.where(t_idx == 0, 0.0, hb)
            hs = hf + hb0
            cols = slice(o * HY_WIDTH, (o + 1) * HY_WIDTH)
            hs_hi[:, cols], hs_lo[:, cols] = _split_bf16(hs)
            hd_hi[:, cols], hd_lo[:, cols] = _split_bf16(hb0 - hf)
            any_ref[0, o] = jnp.sum(hs * sign, axis=0, keepdims=True) * (1.0 / n_fft)

    def dot2(tab, h_hi, h_lo):
        return (jnp.dot(tab[...], h_hi[...], preferred_element_type=F32)
                + jnp.dot(tab[...], h_lo[...], preferred_element_type=F32))

    kr = dot2(c_ref, hs_hi, hs_lo)
    ki = dot2(s_ref, hd_hi, hd_lo)
    f_idx = lax.broadcasted_iota(jnp.int32, (fb, 2 * HY_WIDTH), 0) + j * fb
    wf = jnp.where(f_idx == 0, 1.0 / n_fft, 2.0 / n_fft)
    kr = kr * wf
    ki = ki * wf
    for o in range(2):
        ar_ref[0, o] = kr[:, o * HY_WIDTH:(o + 1) * HY_WIDTH]
        ai_ref[0, o] = ki[:, o * HY_WIDTH:(o + 1) * HY_WIDTH]


def _hyena_filter_call(ztab, w1, b1, f1, w2, b2, f2, w3, deltas, cbf, sbf, seq):
    depth = w1.shape[0]
    fb = min(512, seq)
    tab = pl.BlockSpec((fb, seq), lambda l, j: (j, 0))
    nf = seq // fb
    kern = functools.partial(_hyena_filter_kernel, seq=seq, fb=fb)
    lay = lambda l, j: (l, 0, 0)
    return pl.pallas_call(
        kern,
        out_shape=(jax.ShapeDtypeStruct((depth, 2, seq, HY_WIDTH), F32),
                   jax.ShapeDtypeStruct((depth, 2, seq, HY_WIDTH), F32),
                   jax.ShapeDtypeStruct((depth, 2, 1, HY_WIDTH), F32)),
        grid=(depth, nf),
        in_specs=[
            _const_spec(ztab.shape),
            pl.BlockSpec((1,) + w1.shape[1:], lay), pl.BlockSpec((1,) + b1.shape[1:], lay),
            pl.BlockSpec((1,) + f1.shape[1:], lay),
            pl.BlockSpec((1,) + w2.shape[1:], lay), pl.BlockSpec((1,) + b2.shape[1:], lay),
            pl.BlockSpec((1,) + f2.shape[1:], lay),
            pl.BlockSpec((1,) + w3.shape[1:], lay),
            _const_spec(deltas.shape), tab, tab,
        ],
        out_specs=(pl.BlockSpec((1, 2, fb, HY_WIDTH), lambda l, j: (l, 0, j, 0)),
                   pl.BlockSpec((1, 2, fb, HY_WIDTH), lambda l, j: (l, 0, j, 0)),
                   pl.BlockSpec((1, 2, 1, HY_WIDTH), lambda l, j: (l, 0, 0, 0))),
        scratch_shapes=[pltpu.VMEM((seq, 2 * HY_WIDTH), BF16) for _ in range(4)],
        compiler_params=_cparams(("arbitrary", "arbitrary"), 48),
        name="hyena_filter",
    )(ztab, w1, b1, f1, w2, b2, f2, w3, deltas, cbf, sbf)


def _hyena_kernel(p_ref, cw_ref, cb_ref, c_ref, s_ref, ar_ref, ai_ref, any_ref, sk_ref, o_ref,
                  g_scr, z_scr, zb_scr, acc_scr, *, seq, fb):
    w = HY_WIDTH
    rc = min(ROW_CHUNK, seq)
    load_row = lambda r, n: p_ref[0, r:r + n, :]
    for r0 in range(0, seq, rc):
        x = p_ref[0, r0:r0 + rc, :]
        u = (cw_ref[0:1, :] * _chunk_shift(load_row, x, r0, 1, seq) + cw_ref[1:2, :] * x
             + cw_ref[2:3, :] * _chunk_shift(load_row, x, r0, -1, seq) + cb_ref[...])
        g_scr[r0:r0 + rc, :] = u[:, 0:2 * w]
        z_scr[r0:r0 + rc, :] = u[:, 2 * w:3 * w]
    t_idx = lax.broadcasted_iota(jnp.int32, (seq, w), 0)
    sign = jnp.where(t_idx % 2 == 0, 1.0, -1.0)
    for o in range(2):
        z = z_scr[...]
        zb_scr[...] = z.astype(BF16)
        acc_scr[...] = sign * (jnp.sum(z * sign, axis=0, keepdims=True) * any_ref[0, o])
        nf = seq // fb

        def forward(j):
            rows = slice(j * fb, (j + 1) * fb)
            return (jnp.dot(c_ref[rows, :], zb_scr[...], preferred_element_type=F32),
                    jnp.dot(s_ref[rows, :], zb_scr[...], preferred_element_type=F32))

        pending = [forward(j) for j in range(min(HY_LOOKAHEAD, nf))]
        for j in range(nf):
            if j + HY_LOOKAHEAD < nf:
                pending.append(forward(j + HY_LOOKAHEAD))
            ur, us = pending.pop(0)
            rows = slice(j * fb, (j + 1) * fb)
            ar = ar_ref[0, o, rows, :]
            ai = ai_ref[0, o, rows, :]
            pp = (ur * ar + us * ai).astype(BF16)
            qq = (us * ar - ur * ai).astype(BF16)
            acc_scr[...] += (jnp.dot(c_ref[:, rows], pp, preferred_element_type=F32)
                             + jnp.dot(s_ref[:, rows], qq, preferred_element_type=F32))
        z_new = g_scr[:, o * w:(o + 1) * w] * (acc_scr[...] + sk_ref[0, o] * z_scr[...])
        if o == 0:
            z_scr[...] = z_new
        else:
            o_ref[0] = z_new.astype(o_ref.dtype)


def _hyena_call(proj, cw, cb, cbf, sbf, ar, ai, any_, skip, layer):
    bsz, seq, _ = proj.shape
    fb = min(HY_FREQ_BLOCK, seq)
    kern = functools.partial(_hyena_kernel, seq=seq, fb=fb)
    lsel = lambda b: (layer, 0, 0, 0)
    return pl.pallas_call(
        kern,
        out_shape=jax.ShapeDtypeStruct((bsz, seq, HY_WIDTH), BF16),
        grid=(bsz,),
        in_specs=[
            pl.BlockSpec((1, seq, 3 * HY_WIDTH), lambda b: (b, 0, COL_HY // (3 * HY_WIDTH))),
            _const_spec(cw.shape), _const_spec(cb.shape),
            _const_spec(cbf.shape), _const_spec(sbf.shape),
            pl.BlockSpec((1, 2, seq, HY_WIDTH), lsel, pipeline_mode=pl.Buffered(1)),
            pl.BlockSpec((1, 2, seq, HY_WIDTH), lsel, pipeline_mode=pl.Buffered(1)),
            pl.BlockSpec((1, 2, 1, HY_WIDTH), lsel, pipeline_mode=pl.Buffered(1)),
            pl.BlockSpec((1, 2, 1, HY_WIDTH), lsel, pipeline_mode=pl.Buffered(1)),
        ],
        out_specs=pl.BlockSpec((1, seq, HY_WIDTH), lambda b: (b, 0, 0)),
        scratch_shapes=[pltpu.VMEM((seq, 2 * HY_WIDTH), F32), pltpu.VMEM((seq, HY_WIDTH), F32),
                        pltpu.VMEM((seq, HY_WIDTH), BF16), pltpu.VMEM((seq, HY_WIDTH), F32)],
        compiler_params=_cparams(("parallel",), 60),
        name="hyena",
    )(proj, cw, cb, cbf, sbf, ar, ai, any_, skip)


def _lru_kernel(xr_ref, xg_ref, cw_ref, cb_ref, wg_ref, bg_ref, lam_ref, o_ref, a_scr, u_scr, *, seq):
    w = LRU_WIDTH
    rc = min(ROW_CHUNK, seq)
    lam = lam_ref[...]
    y = jnp.exp(-jnp.abs(lam))
    w1 = 1.0 + y
    log1p_y = jnp.where(w1 == 1.0, y, jnp.log(w1) * (y / jnp.where(w1 == 1.0, 1.0, w1 - 1.0)))
    softplus_neg = jnp.maximum(-lam, 0.0) + log1p_y
    load_row = lambda r, n: xr_ref[0, r:r + n, :]
    t_in = lax.broadcasted_iota(jnp.int32, (rc // SUBLANES, SUBLANES, w), 1)
    for r0 in range(0, seq, rc):
        x = xr_ref[0, r0:r0 + rc, :]
        xc = (cw_ref[0:1, :] * _chunk_shift(load_row, x, r0, 2, seq)
              + cw_ref[1:2, :] * _chunk_shift(load_row, x, r0, 1, seq)
              + cw_ref[2:3, :] * x + cw_ref[3:4, :] * _chunk_shift(load_row, x, r0, -1, seq) + cb_ref[...])
        xcb = xc.astype(BF16)
        for d in range(2):
            ca, cx = slice(d * w, (d + 1) * w), slice((2 + d) * w, (3 + d) * w)
            gate_a = _sigmoid(jnp.dot(xcb, wg_ref[:, ca], preferred_element_type=F32) + bg_ref[:, ca])
            gate_x = _sigmoid(jnp.dot(xcb, wg_ref[:, cx], preferred_element_type=F32) + bg_ref[:, cx])
            a = jnp.exp(-LRU_C * gate_a * softplus_neg[:, ca])
            u = jnp.sqrt(1.0 - a * a) * gate_x * xc
            a = a.reshape(rc // SUBLANES, SUBLANES, w)
            u = u.reshape(rc // SUBLANES, SUBLANES, w)
            s = 1
            while s < SUBLANES:
                keep = (t_in >= s) if d == 0 else (t_in < SUBLANES - s)
                sh = s if d == 0 else SUBLANES - s
                u = u + a * jnp.where(keep, pltpu.roll(u, sh, axis=1), 0.0)
                a = a * jnp.where(keep, pltpu.roll(a, sh, axis=1), 1.0)
                s *= 2
            a_scr[d, r0:r0 + rc, :] = a.reshape(rc, w)
            u_scr[d, r0:r0 + rc, :] = u.reshape(rc, w)

    ng = seq // SUBLANES

    def carry_step(i, carry):
        cf, cb = carry
        rf = pl.multiple_of(i * SUBLANES, SUBLANES)
        rb = pl.multiple_of((ng - 1 - i) * SUBLANES, SUBLANES)
        hf = u_scr[0, pl.ds(rf, SUBLANES), :] + a_scr[0, pl.ds(rf, SUBLANES), :] * cf
        hb = u_scr[1, pl.ds(rb, SUBLANES), :] + a_scr[1, pl.ds(rb, SUBLANES), :] * cb
        u_scr[0, pl.ds(rf, SUBLANES), :] = hf
        u_scr[1, pl.ds(rb, SUBLANES), :] = hb
        return hf[SUBLANES - 1:SUBLANES, :], hb[0:1, :]

    zero = jnp.zeros((1, w), F32)
    lax.fori_loop(0, ng, carry_step, (zero, zero), unroll=4)
    for r0 in range(0, seq, rc):
        rows = slice(r0, r0 + rc)
        o_ref[0, rows, :] = ((u_scr[0, rows, :] + u_scr[1, rows, :])
                             * _gelu_tanh(xg_ref[0, rows, :])).astype(o_ref.dtype)


def _lru_call(proj, cw, cb, wg, bg, lam):
    bsz, seq, _ = proj.shape
    w = LRU_WIDTH
    return pl.pallas_call(
        functools.partial(_lru_kernel, seq=seq),
        scratch_shapes=[pltpu.VMEM((2, seq, w), F32), pltpu.VMEM((2, seq, w), F32)],
        out_shape=jax.ShapeDtypeStruct((bsz, seq, w), BF16),
        grid=(bsz,),
        in_specs=[
            pl.BlockSpec((1, seq, w), lambda b: (b, 0, COL_LR // w)),
            pl.BlockSpec((1, seq, w), lambda b: (b, 0, COL_LG // w)),
            _const_spec(cw.shape), _const_spec(cb.shape), _const_spec(wg.shape),
            _const_spec(bg.shape), _const_spec(lam.shape),
        ],
        out_specs=pl.BlockSpec((1, seq, w), lambda b: (b, 0, 0)),
        compiler_params=_cparams(("parallel",), 48),
        name="rglru",
    )(proj, proj, cw, cb, wg, bg, lam)


def _rope128(x, cos, sin):
    lane = lax.broadcasted_iota(jnp.int32, x.shape, 1)
    half = MLA_ROPE // 2
    partner = jnp.where(lane < KR_LANE + half, pltpu.roll(x, LANE - half, axis=1),
                        pltpu.roll(x, half, axis=1))
    return x * cos + partner * sin


def _mla_prep_kernel(cq_ref, ckv_ref, kr_ref, qg_ref, wq_ref, kg_ref, wk_ref, wvt_ref, cos_ref,
                     sin_ref, q_ref, k_ref, vt_ref, *, qscale):
    cos = cos_ref[...]
    sin = sin_ref[...]
    qn = _rms_norm(cq_ref[0], qg_ref[...]).astype(BF16)
    q = jnp.dot(qn, wq_ref[...], preferred_element_type=F32)
    kvn = _rms_norm(ckv_ref[0], kg_ref[...]).astype(BF16)
    kn = jnp.dot(kvn, wk_ref[...], preferred_element_type=F32)
    v_t = lax.dot_general(wvt_ref[...], kvn, (((1,), (1,)), ((), ())), preferred_element_type=F32)
    row = lax.broadcasted_iota(jnp.int32, v_t.shape, 0)
    vt_ref[0] = jnp.where((row & (HEAD_PAD - 1)) == MLA_V, 1.0, v_t).astype(BF16)
    kr = _rope128(kr_ref[0], cos, sin)
    for h in range(MLA_HEADS):
        cols = slice(h * HEAD_PAD, (h + 1) * HEAD_PAD)
        q_ref[0, :, cols] = (_rope128(q[:, cols], cos, sin) * qscale).astype(BF16)
        k_ref[0, :, cols] = (kn[:, cols] + kr).astype(BF16)


def _mla_prep_call(proj, qg, wq, kg, wk, wvt, cos_t, sin_t):
    bsz, seq, _ = proj.shape
    hp = MLA_HEADS * HEAD_PAD
    tr = min(512, seq)
    qscale = float((MLA_NOPE + MLA_ROPE) ** -0.5 * math.log2(math.e))
    return pl.pallas_call(
        functools.partial(_mla_prep_kernel, qscale=qscale),
        out_shape=(jax.ShapeDtypeStruct((bsz, seq, hp), BF16),
                   jax.ShapeDtypeStruct((bsz, seq, hp), BF16),
                   jax.ShapeDtypeStruct((bsz, hp, seq), BF16)),
        grid=(bsz, seq // tr),
        in_specs=[
            pl.BlockSpec((1, tr, MLA_Q_LORA), lambda b, i: (b, i, COL_CQ // MLA_Q_LORA)),
            pl.BlockSpec((1, tr, MLA_KV_LORA), lambda b, i: (b, i, COL_CKV // MLA_KV_LORA)),
            pl.BlockSpec((1, tr, LANE), lambda b, i: (b, i, COL_KR // LANE)),
            _const_spec(qg.shape), _const_spec(wq.shape), _const_spec(kg.shape),
            _const_spec(wk.shape), _const_spec(wvt.shape),
            pl.BlockSpec((tr, LANE), lambda b, i: (i, 0)), pl.BlockSpec((tr, LANE), lambda b, i: (i, 0)),
        ],
        out_specs=(pl.BlockSpec((1, tr, hp), lambda b, i: (b, i, 0)),
                   pl.BlockSpec((1, tr, hp), lambda b, i: (b, i, 0)),
                   pl.BlockSpec((1, hp, tr), lambda b, i: (b, 0, i))),
        compiler_params=_cparams(("parallel", "parallel"), 48),
        name="mla_prep",
    )(proj, proj, proj, qg, wq, kg, wk, wvt, cos_t, sin_t)


def _attn_kernel(q_ref, k_ref, vt_ref, o_ref, s_scr, p_scr):
    tq = q_ref.shape[1]
    ct = min(ATTN_COL_TILE, tq)
    chains = [(h, c0) for c0 in range(0, tq, ct) for h in range(2)]
    nt = (((1,), (1,)), ((), ()))

    ns = s_scr.shape[0]

    def scores(i):
        h, c0 = chains[i]
        cols = slice(h * HEAD_PAD, (h + 1) * HEAD_PAD)
        s_scr[i % ns] = lax.dot_general(k_ref[0, :, cols], q_ref[0, c0:c0 + ct, cols], nt,
                                        preferred_element_type=F32)

    def finish(i):
        h = chains[i][0]
        s_t = s_scr[i % ns]
        p_scr[i % 2] = jnp.exp2(s_t - jnp.max(s_t, axis=0, keepdims=True)).astype(BF16)
        o_t = jnp.dot(vt_ref[0, h * HEAD_PAD:(h + 1) * HEAD_PAD, :], p_scr[i % 2],
                      preferred_element_type=F32)
        return o_t[0:MLA_V, :] / o_t[MLA_V:MLA_V + 1, :]

    for i in range(min(ATTN_LOOKAHEAD, len(chains))):
        scores(i)
    done = {}
    for i, (h, c0) in enumerate(chains):
        if i + ATTN_LOOKAHEAD < len(chains):
            scores(i + ATTN_LOOKAHEAD)
        done[h] = finish(i)
        if h == 1:
            o_ref[0, c0:c0 + ct, :] = jnp.concatenate([done[0], done[1]], axis=0).T.astype(o_ref.dtype)


def _attn_call(q, k, vt):
    bsz, seq, hp = q.shape
    tq = min(2048, seq)
    pw = 2 * HEAD_PAD
    return pl.pallas_call(
        _attn_kernel,
        out_shape=jax.ShapeDtypeStruct((bsz, seq, MLA_HEADS * MLA_V), BF16),
        grid=(bsz, MLA_HEADS // 2, seq // tq),
        in_specs=[pl.BlockSpec((1, tq, pw), lambda b, j, i: (b, i, j)),
                  pl.BlockSpec((1, seq, pw), lambda b, j, i: (b, 0, j)),
                  pl.BlockSpec((1, pw, seq), lambda b, j, i: (b, j, 0))],
        out_specs=pl.BlockSpec((1, tq, HEAD_PAD), lambda b, j, i: (b, i, j)),
        scratch_shapes=[pltpu.VMEM((ATTN_LOOKAHEAD + 1, seq, min(ATTN_COL_TILE, tq)), F32),
                        pltpu.VMEM((2, seq, min(ATTN_COL_TILE, tq)), BF16)],
        compiler_params=_cparams(("parallel", "parallel", "parallel"), 48),
        name="mla_attn",
    )(q, k, vt)


def _postmix_kernel(yh_ref, yl_ref, ya_ref, x_ref, gn_ref, wo_ref, g_ref, b_ref, o_ref, ob_ref):
    g0, g1 = HY_WIDTH, HY_WIDTH + LRU_WIDTH
    y = jnp.concatenate([_rms_norm(yh_ref[...].astype(F32), gn_ref[:, 0:g0]),
                         _rms_norm(yl_ref[...].astype(F32), gn_ref[:, g0:g1]),
                         _rms_norm(ya_ref[...].astype(F32), gn_ref[:, g1:])], axis=-1).astype(BF16)
    mix = jnp.dot(y, wo_ref[...], preferred_element_type=F32)
    x1 = _layer_norm(ALPHA * x_ref[...] + mix, g_ref[...], b_ref[...])
    o_ref[...] = x1
    ob_ref[...] = x1.astype(BF16)


def _postmix_call(yh, yl, ya, x2d, gn, wo, g, b, tm):
    t, d = x2d.shape
    row = lambda w: pl.BlockSpec((tm, w), lambda i: (i, 0))
    return pl.pallas_call(
        _postmix_kernel,
        out_shape=(jax.ShapeDtypeStruct((t, d), F32), jax.ShapeDtypeStruct((t, d), BF16)),
        grid=(t // tm,),
        in_specs=[row(HY_WIDTH), row(LRU_WIDTH), row(MLA_HEADS * MLA_V), row(d),
                  _const_spec(gn.shape), _const_spec(wo.shape), _const_spec(g.shape), _const_spec(b.shape)],
        out_specs=(row(d), row(d)),
        compiler_params=_cparams(("parallel",), 48),
        name="postmix",
    )(yh, yl, ya, x2d, gn, wo, g, b)


def _cumsum_lanes(x):
    n = x.shape[1]
    lane = lax.broadcasted_iota(jnp.int32, x.shape, 1)
    d = 1
    while d < n:
        x = x + jnp.where(lane >= d, pltpu.roll(x, d, axis=1), 0.0)
        d *= 2
    return x


def _select_kernel(x_ref, rw_ref, pos_ref, post_ref, gatet_ref, *, cap):
    x = x_ref[0]
    x_hi, x_lo = _split_bf16(x)
    w_hi, w_lo = _split_bf16(rw_ref[...])
    logits = (jnp.dot(x_hi, w_hi, preferred_element_type=F32)
              + (jnp.dot(x_hi, w_lo, preferred_element_type=F32)
                 + jnp.dot(x_lo, w_hi, preferred_element_type=F32)))
    lane = lax.broadcasted_iota(jnp.int32, logits.shape, 1)
    valid = lane < N_EXPERTS
    logits = jnp.where(valid, logits, -1e30)
    m = jnp.max(logits, axis=-1, keepdims=True)
    ex = jnp.where(valid, jnp.exp(logits - m), 0.0)
    aff = ex / jnp.sum(ex, axis=-1, keepdims=True)
    seq = aff.shape[0]
    aff_e = aff.T[0:N_EXPERTS, :]

    def body(_, carry):
        lo, hi = carry
        mid = lo + ((hi - lo + 1) >> 1)
        cnt = jnp.sum(jnp.where(aff_e >= pltpu.bitcast(mid, F32), 1.0, 0.0), axis=1, keepdims=True)
        ok = cnt >= float(cap)
        return jnp.where(ok, mid, lo), jnp.where(ok, hi, mid - 1)

    lo0 = jnp.zeros((N_EXPERTS, 1), jnp.int32)
    hi0 = jnp.full((N_EXPERTS, 1), 0x7F7FFFFF, jnp.int32)
    thr_bits, _ = lax.fori_loop(0, 31, body, (lo0, hi0))
    thr = pltpu.bitcast(thr_bits, F32)
    gt = aff_e > thr
    eq = aff_e == thr
    need = float(cap) - jnp.sum(jnp.where(gt, 1.0, 0.0), axis=1, keepdims=True)
    eq_rank = _cumsum_lanes(jnp.where(eq, 1.0, 0.0))
    sel = gt | (eq & (eq_rank <= need))
    pos = jnp.where(sel, _cumsum_lanes(jnp.where(sel, 1.0, 0.0)) - 1.0, -1.0)
    pos_ref[0] = pos
    fill = jnp.full((LANE - N_EXPERTS, seq), -1.0, F32)
    post_ref[0] = jnp.concatenate([pos, fill], axis=0).T
    gatet_ref[0] = jnp.concatenate([jnp.where(sel, aff_e, 0.0), fill], axis=0).T


def _select_call(x1, rw, cap):
    bsz, seq, d = x1.shape
    kern = functools.partial(_select_kernel, cap=cap)
    return pl.pallas_call(
        kern,
        out_shape=(jax.ShapeDtypeStruct((bsz, N_EXPERTS, seq), F32),
                   jax.ShapeDtypeStruct((bsz, seq, LANE), F32),
                   jax.ShapeDtypeStruct((bsz, seq, LANE), F32)),
        grid=(bsz,),
        in_specs=[pl.BlockSpec((1, seq, d), lambda b: (b, 0, 0)), _const_spec(rw.shape)],
        out_specs=(pl.BlockSpec((1, N_EXPERTS, seq), lambda b: (b, 0, 0)),
                   pl.BlockSpec((1, seq, LANE), lambda b: (b, 0, 0)),
                   pl.BlockSpec((1, seq, LANE), lambda b: (b, 0, 0))),
        compiler_params=_cparams(("parallel",), 48),
        name="moe_select",
    )(x1, rw)


def _gather_kernel(pos_ref, x_ref, o_ref, p_scr, *, cap):
    seq, d = x_ref.shape[1], x_ref.shape[2]
    slot = lax.broadcasted_iota(jnp.int32, (cap, seq), 0).astype(F32)
    for e in range(N_EXPERTS):
        p_scr[e * cap:(e + 1) * cap, :] = jnp.where(pos_ref[0, e:e + 1, :] == slot, 1.0, 0.0).astype(BF16)
    cw = min(MXU_N, d)
    for c in range(0, d, cw):
        o_ref[0, :, c:c + cw] = jnp.dot(p_scr[...], x_ref[0, :, c:c + cw],
                                        preferred_element_type=F32).astype(BF16)


def _gather_call(pos, x1b, cap):
    bsz, seq, d = x1b.shape
    return pl.pallas_call(
        functools.partial(_gather_kernel, cap=cap),
        out_shape=jax.ShapeDtypeStruct((bsz, N_EXPERTS * cap, d), BF16),
        grid=(bsz,),
        in_specs=[pl.BlockSpec((1, N_EXPERTS, seq), lambda b: (b, 0, 0)),
                  pl.BlockSpec((1, seq, d), lambda b: (b, 0, 0))],
        out_specs=pl.BlockSpec((1, N_EXPERTS * cap, d), lambda b: (b, 0, 0)),
        scratch_shapes=[pltpu.VMEM((N_EXPERTS * cap, seq), BF16)],
        compiler_params=_cparams(("parallel",), 56),
        name="moe_gather",
    )(pos, x1b)


def _ffn_kernel(xe_ref, wg_ref, wu_ref, wd_ref, o_ref, acc_scr, *, rows_per_dot):
    f = pl.program_id(1)
    bsz, cap, d = xe_ref.shape
    wcast = lambda ref: ref[0, 0].astype(BF16)
    nb = rows_per_dot // cap

    @pl.when(f == 0)
    def _():
        acc_scr[...] = jnp.zeros_like(acc_scr)

    def up(b0):
        xe = xe_ref[b0:b0 + nb].reshape(nb * cap, d)
        return (jnp.dot(xe, wcast(wg_ref), preferred_element_type=F32),
                jnp.dot(xe, wcast(wu_ref), preferred_element_type=F32))

    starts = list(range(0, bsz, nb))
    nxt = up(starts[0])
    for i, b0 in enumerate(starts):
        hg, hu = nxt
        if i + 1 < len(starts):
            nxt = up(starts[i + 1])
        hid = (hg * _sigmoid(hg) * hu).astype(BF16)
        rows = slice(b0 * cap, (b0 + nb) * cap)
        acc_scr[rows, :] += jnp.dot(hid, wcast(wd_ref), preferred_element_type=F32)

    @pl.when(f == pl.num_programs(1) - 1)
    def _():
        o_ref[...] = acc_scr[...].reshape(bsz, cap, d).astype(BF16)


def _ffn_call(xe, wg, wu, wd, layer, cap):
    bsz, _, d = xe.shape
    _, ne, _, ff = wg.shape
    fcw = min(512, ff)
    rows_per_dot = cap * max(1, min(bsz, 1024 // cap))
    return pl.pallas_call(
        functools.partial(_ffn_kernel, rows_per_dot=rows_per_dot),
        out_shape=jax.ShapeDtypeStruct((bsz, ne * cap, d), BF16),
        grid=(ne, ff // fcw),
        in_specs=[pl.BlockSpec((bsz, cap, d), lambda e, f: (0, e, 0)),
                  pl.BlockSpec((1, 1, d, fcw), lambda e, f: (layer, e, 0, f)),
                  pl.BlockSpec((1, 1, d, fcw), lambda e, f: (layer, e, 0, f)),
                  pl.BlockSpec((1, 1, fcw, d), lambda e, f: (layer, e, f, 0))],
        out_specs=pl.BlockSpec((bsz, cap, d), lambda e, f: (0, e, 0)),
        scratch_shapes=[pltpu.VMEM((bsz * cap, d), F32)],
        compiler_params=_cparams(("parallel", "arbitrary"), 56),
        name="moe_ffn",
    )(xe, wg, wu, wd)


def _combine_kernel(post_ref, gatet_ref, ye_ref, x_ref, g_ref, b_ref, o_ref, *, cap):
    tr = x_ref.shape[1]
    slot = lax.broadcasted_iota(jnp.int32, (tr, cap), 1).astype(F32)
    post = post_ref[0]
    gatet = gatet_ref[0]
    acc = ALPHA * x_ref[0]
    for e in range(N_EXPERTS):
        scat = jnp.where(post[:, e:e + 1] == slot, gatet[:, e:e + 1], 0.0).astype(BF16)
        acc = acc + jnp.dot(scat, ye_ref[0, e * cap:(e + 1) * cap, :], preferred_element_type=F32)
    o_ref[0] = _layer_norm(acc, g_ref[...], b_ref[...])


def _combine_call(post, gatet, ye, x1, g, b, cap):
    bsz, seq, d = x1.shape
    tr = min(1024, seq)
    kern = functools.partial(_combine_kernel, cap=cap)
    return pl.pallas_call(
        kern,
        out_shape=jax.ShapeDtypeStruct((bsz, seq, d), F32),
        grid=(bsz, seq // tr),
        in_specs=[pl.BlockSpec((1, tr, LANE), lambda bi, i: (bi, i, 0)),
                  pl.BlockSpec((1, tr, LANE), lambda bi, i: (bi, i, 0)),
                  pl.BlockSpec((1, N_EXPERTS * cap, d), lambda bi, i: (bi, 0, 0)),
                  pl.BlockSpec((1, tr, d), lambda bi, i: (bi, i, 0)),
                  _const_spec(g.shape), _const_spec(b.shape)],
        out_specs=pl.BlockSpec((1, tr, d), lambda bi, i: (bi, i, 0)),
        compiler_params=_cparams(("parallel", "parallel"), 56),
        name="moe_combine",
    )(post, gatet, ye, x1, g, b)


def _hyena_pos_table(seq):
    t = jnp.linspace(0.0, 1.0, seq, dtype=F32)[:, None]
    bands = (HY_POS_EMB - 1) // 2
    t_idx = jnp.arange(seq, dtype=F32)[:, None]
    freqs = jnp.linspace(1e-4, bands - 1, bands, dtype=F32)[None, :]
    ang = 2.0 * math.pi * t_idx * freqs / seq
    z = jnp.concatenate([t, jnp.cos(ang), -jnp.sin(ang)], axis=-1)
    return jnp.pad(z, ((0, 0), (0, LANE - HY_POS_EMB)))


def _rope_lane_tables(seq):
    half = MLA_ROPE // 2
    inv = ROPE_THETA ** (-jnp.arange(0, MLA_ROPE, 2, dtype=F32) / MLA_ROPE)
    ang = jnp.arange(seq, dtype=F32)[:, None] * inv[None, :]
    cos, sin = jnp.cos(ang), jnp.sin(ang)
    ones = jnp.ones((seq, KR_LANE), F32)
    zeros = jnp.zeros((seq, KR_LANE), F32)
    tail = LANE - KR_LANE - MLA_ROPE
    cos_t = jnp.concatenate([ones, cos, cos, jnp.ones((seq, tail), F32)], axis=-1)
    sin_t = jnp.concatenate([zeros, -sin, sin, jnp.zeros((seq, tail), F32)], axis=-1)
    return cos_t, sin_t


def _relayout_mla(w_uq, w_ukv):
    depth = w_uq.shape[0]
    qd = MLA_NOPE + MLA_ROPE
    wq = w_uq.reshape(depth, MLA_Q_LORA, MLA_HEADS, qd)
    wq = jnp.pad(wq, ((0, 0), (0, 0), (0, 0), (0, HEAD_PAD - qd)))
    wq = wq.reshape(depth, MLA_Q_LORA, MLA_HEADS * HEAD_PAD).astype(BF16)
    wkv = w_ukv.reshape(depth, MLA_KV_LORA, MLA_HEADS, MLA_NOPE + MLA_V)
    wk = jnp.pad(wkv[..., :MLA_NOPE], ((0, 0), (0, 0), (0, 0), (0, HEAD_PAD - MLA_NOPE)))
    wk = wk.reshape(depth, MLA_KV_LORA, MLA_HEADS * HEAD_PAD).astype(BF16)
    wv = jnp.pad(wkv[..., MLA_NOPE:], ((0, 0), (0, 0), (0, 0), (0, HEAD_PAD - MLA_V)))
    wvt = jnp.swapaxes(wv.reshape(depth, MLA_KV_LORA, MLA_HEADS * HEAD_PAD), 1, 2).astype(BF16)
    return wq, wk, wvt


def _relayout_lru_gates(wa, wx, ba, bx):
    depth = wa.shape[0]
    bw = LRU_WIDTH // LRU_BLOCKS
    eye = jnp.eye(LRU_BLOCKS, dtype=wa.dtype)

    def bd(w):
        full = jnp.einsum("ldnjk,nm->ldnjmk", w, eye)
        return full.reshape(depth, 2, LRU_WIDTH, LRU_WIDTH)

    a, x = bd(wa), bd(wx)
    wg = jnp.concatenate([a[:, 0], a[:, 1], x[:, 0], x[:, 1]], axis=-1).astype(BF16)
    bg = jnp.concatenate([ba[:, 0], ba[:, 1], bx[:, 0], bx[:, 1]], axis=-1)[:, None, :]
    del bw
    return wg, bg


def kernel(x, ln_in_g, ln_in_b, w_in, hy_conv_w, hy_conv_b, hy_ffn_w1, hy_ffn_b1, hy_sin_f1,
           hy_ffn_w2, hy_ffn_b2, hy_sin_f2, hy_ffn_w3, hy_skip, lru_conv_w, lru_conv_b, lru_wa, lru_ba,
           lru_wx, lru_bx, lru_lambda, mla_q_norm_g, mla_w_uq, mla_kv_norm_g, mla_w_ukv, group_norm_g,
           w_out, ln_mix_g, ln_mix_b, router_w, exp_w_gate, exp_w_up, exp_w_down, ln_ffn_g, ln_ffn_b):
    bsz, seq, d = x.shape
    depth = w_in.shape[0]
    t = bsz * seq
    tm = min(512, t)
    cap = max(1, EC_CAPACITY_FACTOR * seq // N_EXPERTS)
    row = lambda v: v.reshape(1, -1)

    cbf, sbf = _dft_table_call(seq)
    ztab = _hyena_pos_table(seq)
    max_decay = math.log(HY_DECAY_TARGET) / HY_FAST_DECAY
    min_decay = math.log(HY_DECAY_TARGET) / HY_SLOW_DECAY
    deltas = jnp.abs(jnp.linspace(min_decay, max_decay, HY_WIDTH, dtype=F32))[None, :]
    cos_t, sin_t = _rope_lane_tables(seq)
    wq, wk, wvt = _relayout_mla(mla_w_uq, mla_w_ukv)
    wg_lru, bg_lru = _relayout_lru_gates(lru_wa, lru_wx, lru_ba, lru_bx)
    w1p = jnp.pad(hy_ffn_w1, ((0, 0), (0, LANE - HY_POS_EMB), (0, 0)))
    rw_p = jnp.pad(router_w, ((0, 0), (0, 0), (0, LANE - N_EXPERTS)))
    w_out_b = w_out.astype(BF16)

    ar, ai, any_ = _hyena_filter_call(ztab, w1p, hy_ffn_b1[:, None, :], hy_sin_f1[:, None, :], hy_ffn_w2,
                                      hy_ffn_b2[:, None, :], hy_sin_f2[:, None, :], hy_ffn_w3, deltas,
                                      cbf, sbf, seq)
    skip = hy_skip[:, :, None, :]

    xc = _ln_call(x.reshape(t, d), row(ln_in_g), row(ln_in_b), tm)
    for l in range(depth):
        proj = _proj_call(xc, w_in, l, tm).reshape(bsz, seq, PROJ_W)
        y_hy = _hyena_call(proj, hy_conv_w[l], row(hy_conv_b[l]), cbf, sbf, ar, ai, any_, skip, l)
        y_lr = _lru_call(proj, lru_conv_w[l], row(lru_conv_b[l]), wg_lru[l], bg_lru[l],
                         lru_lambda[l].reshape(1, -1))
        q, k, vt = _mla_prep_call(proj, row(mla_q_norm_g[l]), wq[l], row(mla_kv_norm_g[l]), wk[l], wvt[l],
                                  cos_t, sin_t)
        y_at = _attn_call(q, k, vt)
        x1, x1b = _postmix_call(y_hy.reshape(t, -1), y_lr.reshape(t, -1), y_at.reshape(t, -1), xc,
                                row(group_norm_g[l]), w_out_b[l], row(ln_mix_g[l]), row(ln_mix_b[l]), tm)
        x1 = x1.reshape(bsz, seq, d)
        pos, post, gatet = _select_call(x1, rw_p[l], cap)
        xe = _gather_call(pos, x1b.reshape(bsz, seq, d), cap)
        ye = _ffn_call(xe, exp_w_gate, exp_w_up, exp_w_down, l, cap)
        xc = _combine_call(post, gatet, ye, x1, row(ln_ffn_g[l]), row(ln_ffn_b[l]), cap).reshape(t, d)
    return xc.reshape(bsz, seq, d)
```

```python
import functools
import math

import jax
import jax.numpy as jnp
from jax import lax
from jax.experimental import pallas as pl
from jax.experimental.pallas import tpu as pltpu

F32 = jnp.float32
BF16 = jnp.bfloat16
HIGHEST = lax.Precision.HIGHEST

D_MODEL = 1024
HY_WIDTH = 256
LRU_WIDTH = 256
LRU_BLOCKS = 4
MLA_HEADS = 8
MLA_NOPE = 64
MLA_ROPE = 32
MLA_V = 64
MLA_Q_LORA = 768
MLA_KV_LORA = 256
ROPE_THETA = 10000.0
HY_POS_EMB = 33
HY_FILTER_FFN = 64
HY_FAST_DECAY = 0.3
HY_SLOW_DECAY = 1.5
HY_DECAY_TARGET = 1e-2
LRU_C = 8.0
N_EXPERTS = 16
EC_CAPACITY_FACTOR = 2
EPS = 1e-5
DEPTH = 4
ALPHA = (2.0 * DEPTH) ** 0.25

LANE = 128
MXU_N = 256
SUBLANES = 8
ROW_CHUNK = 256
HY_FREQ_BLOCK = 256
HY_LOOKAHEAD = 2
ATTN_COL_TILE = 256
ATTN_LOOKAHEAD = 4
ATTN_V_ROWS = MLA_V + 16
HEAD_PAD = 128
PROJ_W = 2432
COL_HY, COL_CQ, COL_LR, COL_LG, COL_CKV, COL_KR = 0, 768, 1536, 1792, 2048, 2304
KR_LANE = 64
VMEM_CAP = 64 * 1024 * 1024


def _cparams(sem, vmem_mb):
    return pltpu.CompilerParams(dimension_semantics=sem, vmem_limit_bytes=vmem_mb * 1024 * 1024)


def _const_spec(shape):
    nd = len(shape)
    return pl.BlockSpec(shape, lambda *_: (0,) * nd, pipeline_mode=pl.Buffered(1))


def _layer_norm(x, g, b):
    xc = x - jnp.mean(x, axis=-1, keepdims=True)
    var = jnp.mean(xc * xc, axis=-1, keepdims=True)
    return xc * lax.rsqrt(var + EPS) * g + b


def _rms_norm(x, g):
    return x * lax.rsqrt(jnp.mean(x * x, axis=-1, keepdims=True) + EPS) * g


def _chunk_shift(load_rows, x, r0, d, seq):
    rc = x.shape[0]
    if d == 0:
        return x
    if 0 <= r0 - d and r0 - d + rc <= seq:
        return load_rows(r0 - d, rc)
    load_row = lambda r: load_rows(r, 1)
    t = lax.broadcasted_iota(jnp.int32, x.shape, 0)
    y = pltpu.roll(x, d % rc, axis=0)
    for k in range(abs(d)):
        dst = k if d > 0 else rc + d + k
        src = r0 - d + dst
        row = load_row(src) if 0 <= src < seq else jnp.zeros((1, x.shape[1]), x.dtype)
        y = jnp.where(t == dst, row, y)
    return y


def _sigmoid(x):
    return 1.0 / (1.0 + jnp.exp(-x))


def _gelu_tanh(x):
    c = math.sqrt(2.0 / math.pi)
    return 0.5 * x * (1.0 + jnp.tanh(c * (x + 0.044715 * (x * x * x))))


def _ln_kernel(x_ref, g_ref, b_ref, o_ref):
    o_ref[...] = _layer_norm(x_ref[...], g_ref[...], b_ref[...])


def _ln_call(x2d, g, b, tm):
    t, d = x2d.shape
    return pl.pallas_call(
        _ln_kernel,
        out_shape=jax.ShapeDtypeStruct((t, d), F32),
        grid=(t // tm,),
        in_specs=[pl.BlockSpec((tm, d), lambda i: (i, 0)), _const_spec((1, d)), _const_spec((1, d))],
        out_specs=pl.BlockSpec((tm, d), lambda i: (i, 0)),
        compiler_params=_cparams(("parallel",), 32),
        name="ln_in",
    )(x2d, g, b)


PROJ_COL_MAP = ((0, 768, COL_HY), (768, 256, COL_LR), (1024, 256, COL_LG), (1280, 768, COL_CQ),
                (2048, 256, COL_CKV))
PROJ_KR_SRC = 2304


def _proj_kernel(x_ref, w_ref, o_ref, w_scr):
    @pl.when(pl.program_id(0) == 0)
    def _():
        for src, width, dst in PROJ_COL_MAP:
            w_scr[:, dst:dst + width] = w_ref[0, :, src:src + width].astype(BF16)
        kr = w_ref[0, :, PROJ_KR_SRC:PROJ_KR_SRC + MLA_ROPE]
        rows = kr.shape[0]
        w_scr[:, COL_KR:COL_KR + LANE] = jnp.concatenate(
            [jnp.zeros((rows, KR_LANE), F32), kr, jnp.zeros((rows, LANE - KR_LANE - MLA_ROPE), F32)],
            axis=1).astype(BF16)

    o_ref[...] = jnp.dot(x_ref[...].astype(BF16), w_scr[...], preferred_element_type=F32)


def _proj_call(x2d, w_in, layer, tm):
    t, d = x2d.shape
    n_in = w_in.shape[2]
    return pl.pallas_call(
        _proj_kernel,
        out_shape=jax.ShapeDtypeStruct((t, PROJ_W), F32),
        grid=(t // tm,),
        in_specs=[pl.BlockSpec((tm, d), lambda i: (i, 0)),
                  pl.BlockSpec((1, d, n_in), lambda i: (layer, 0, 0), pipeline_mode=pl.Buffered(1))],
        out_specs=pl.BlockSpec((tm, PROJ_W), lambda i: (i, 0)),
        scratch_shapes=[pltpu.VMEM((d, PROJ_W), BF16)],
        compiler_params=_cparams(("arbitrary",), 48),
        name="proj",
    )(x2d, w_in)


def _split_bf16(x):
    hi = x.astype(BF16)
    return hi, (x - hi.astype(F32)).astype(BF16)


def _dft_table_kernel(c_ref, s_ref, tc_scr, ts_scr, *, seq, rb):
    j = pl.program_id(0)
    mask = 2 * seq - 1
    step = math.pi / seq

    @pl.when(j == 0)
    def _():
        fi = lax.broadcasted_iota(jnp.int32, (rb, seq), 0)
        t = lax.broadcasted_iota(jnp.int32, (rb, seq), 1)
        ang = ((fi * t) & mask).astype(F32) * step
        tc_scr[...] = jnp.cos(ang)
        ts_scr[...] = jnp.sin(ang)

    t = lax.broadcasted_iota(jnp.int32, (SUBLANES, seq), 1)
    ang = (((j * rb) * t) & mask).astype(F32) * step
    ca = jnp.cos(ang)[0:1, :]
    sa = jnp.sin(ang)[0:1, :]
    c = ca * tc_scr[...] - sa * ts_scr[...]
    s = sa * tc_scr[...] + ca * ts_scr[...]
    c_ref[...] = c.astype(BF16)
    s_ref[...] = s.astype(BF16)


def _dft_table_call(seq):
    assert seq & (seq - 1) == 0, "sequence length must be a power of two"
    rb = min(256, seq)
    blk = pl.BlockSpec((rb, seq), lambda j: (j, 0))
    return pl.pallas_call(
        functools.partial(_dft_table_kernel, seq=seq, rb=rb),
        out_shape=tuple(jax.ShapeDtypeStruct((seq, seq), BF16) for _ in range(2)),
        grid=(seq // rb,),
        in_specs=[],
        out_specs=(blk, blk),
        scratch_shapes=[pltpu.VMEM((rb, seq), F32), pltpu.VMEM((rb, seq), F32)],
        compiler_params=_cparams(("arbitrary",), 40),
        name="dft_tables",
    )()


def _hyena_filter_kernel(z_ref, w1_ref, b1_ref, f1_ref, w2_ref, b2_ref, f2_ref, w3_ref, dl_ref,
                         c_ref, s_ref, ar_ref, ai_ref, any_ref,
                         hs_hi, hs_lo, hd_hi, hd_lo, *, seq, fb):
    j = pl.program_id(1)
    n_fft = 2 * seq

    @pl.when(j == 0)
    def _():
        z = z_ref[...]
        h = jnp.sin(f1_ref[0] * (jnp.dot(z, w1_ref[0], precision=HIGHEST, preferred_element_type=F32)
                                 + b1_ref[0]))
        h = jnp.sin(f2_ref[0] * (jnp.dot(h, w2_ref[0], precision=HIGHEST, preferred_element_type=F32)
                                 + b2_ref[0]))
        t_idx = lax.broadcasted_iota(jnp.int32, (seq, HY_WIDTH), 0)
        t = t_idx.astype(F32) * (1.0 / (seq - 1))
        window = jnp.exp(-t * dl_ref[...])
        sign = jnp.where(t_idx % 2 == 0, 1.0, -1.0)

        def filt(g):
            w3g = w3_ref[0, :, g * HY_WIDTH:(g + 1) * HY_WIDTH]
            return jnp.dot(h, w3g, precision=HIGHEST, preferred_element_type=F32) * window

        for o in range(2):
            hf = filt(o)
            hb = filt(2 + o)
            hb0 = jnp.where(t_idx == 0, 0.0, hb)
            hs = hf + hb0
            cols = slice(o * HY_WIDTH, (o + 1) * HY_WIDTH)
            hs_hi[:, cols], hs_lo[:, cols] = _split_bf16(hs)
            hd_hi[:, cols], hd_lo[:, cols] = _split_bf16(hb0 - hf)
            any_ref[0, o] = jnp.sum(hs * sign, axis=0, keepdims=True) * (1.0 / n_fft)

    def dot2(tab, h_hi, h_lo):
        return (jnp.dot(tab[...], h_hi[...], preferred_element_type=F32)
                + jnp.dot(tab[...], h_lo[...], preferred_element_type=F32))

    kr = dot2(c_ref, hs_hi, hs_lo)
    ki = dot2(s_ref, hd_hi, hd_lo)
    f_idx = lax.broadcasted_iota(jnp.int32, (fb, 2 * HY_WIDTH), 0) + j * fb
    wf = jnp.where(f_idx == 0, 1.0 / n_fft, 2.0 / n_fft)
    kr = kr * wf
    ki = ki * wf
    for o in range(2):
        ar_ref[0, o] = kr[:, o * HY_WIDTH:(o + 1) * HY_WIDTH]
        ai_ref[0, o] = ki[:, o * HY_WIDTH:(o + 1) * HY_WIDTH]


def _hyena_filter_call(ztab, w1, b1, f1, w2, b2, f2, w3, deltas, cbf, sbf, seq):
    depth = w1.shape[0]
    fb = min(512, seq)
    tab = pl.BlockSpec((fb, seq), lambda l, j: (j, 0))
    nf = seq // fb
    kern = functools.partial(_hyena_filter_kernel, seq=seq, fb=fb)
    lay = lambda l, j: (l, 0, 0)
    return pl.pallas_call(
        kern,
        out_shape=(jax.ShapeDtypeStruct((depth, 2, seq, HY_WIDTH), F32),
                   jax.ShapeDtypeStruct((depth, 2, seq, HY_WIDTH), F32),
                   jax.ShapeDtypeStruct((depth, 2, 1, HY_WIDTH), F32)),
        grid=(depth, nf),
        in_specs=[
            _const_spec(ztab.shape),
            pl.BlockSpec((1,) + w1.shape[1:], lay), pl.BlockSpec((1,) + b1.shape[1:], lay),
            pl.BlockSpec((1,) + f1.shape[1:], lay),
            pl.BlockSpec((1,) + w2.shape[1:], lay), pl.BlockSpec((1,) + b2.shape[1:], lay),
            pl.BlockSpec((1,) + f2.shape[1:], lay),
            pl.BlockSpec((1,) + w3.shape[1:], lay),
            _const_spec(deltas.shape), tab, tab,
        ],
        out_specs=(pl.BlockSpec((1, 2, fb, HY_WIDTH), lambda l, j: (l, 0, j, 0)),
                   pl.BlockSpec((1, 2, fb, HY_WIDTH), lambda l, j: (l, 0, j, 0)),
                   pl.BlockSpec((1, 2, 1, HY_WIDTH), lambda l, j: (l, 0, 0, 0))),
        scratch_shapes=[pltpu.VMEM((seq, 2 * HY_WIDTH), BF16) for _ in range(4)],
        compiler_params=_cparams(("arbitrary", "arbitrary"), 48),
        name="hyena_filter",
    )(ztab, w1, b1, f1, w2, b2, f2, w3, deltas, cbf, sbf)


def _hyena_kernel(p_ref, cw_ref, cb_ref, c_ref, s_ref, ar_ref, ai_ref, any_ref, sk_ref, o_ref,
                  g_scr, z_scr, zb_scr, acc_scr, *, seq, fb):
    w = HY_WIDTH
    rc = min(ROW_CHUNK, seq)
    load_row = lambda r, n: p_ref[0, r:r + n, :]
    for r0 in range(0, seq, rc):
        x = p_ref[0, r0:r0 + rc, :]
        u = (cw_ref[0:1, :] * _chunk_shift(load_row, x, r0, 1, seq) + cw_ref[1:2, :] * x
             + cw_ref[2:3, :] * _chunk_shift(load_row, x, r0, -1, seq) + cb_ref[...])
        g_scr[r0:r0 + rc, :] = u[:, 0:2 * w]
        z_scr[r0:r0 + rc, :] = u[:, 2 * w:3 * w]
    t_idx = lax.broadcasted_iota(jnp.int32, (seq, w), 0)
    sign = jnp.where(t_idx % 2 == 0, 1.0, -1.0)
    for o in range(2):
        z = z_scr[...]
        zb_scr[...] = z.astype(BF16)
        acc_scr[...] = sign * (jnp.sum(z * sign, axis=0, keepdims=True) * any_ref[0, o])
        nf = seq // fb

        def forward(j):
            rows = slice(j * fb, (j + 1) * fb)
            return (jnp.dot(c_ref[rows, :], zb_scr[...], preferred_element_type=F32),
                    jnp.dot(s_ref[rows, :], zb_scr[...], preferred_element_type=F32))

        pending = [forward(j) for j in range(min(HY_LOOKAHEAD, nf))]
        for j in range(nf):
            if j + HY_LOOKAHEAD < nf:
                pending.append(forward(j + HY_LOOKAHEAD))
            ur, us = pending.pop(0)
            rows = slice(j * fb, (j + 1) * fb)
            ar = ar_ref[0, o, rows, :]
            ai = ai_ref[0, o, rows, :]
            pp = (ur * ar + us * ai).astype(BF16)
            qq = (us * ar - ur * ai).astype(BF16)
            acc_scr[...] += (jnp.dot(c_ref[:, rows], pp, preferred_element_type=F32)
                             + jnp.dot(s_ref[:, rows], qq, preferred_element_type=F32))
        z_new = g_scr[:, o * w:(o + 1) * w] * (acc_scr[...] + sk_ref[0, o] * z_scr[...])
        if o == 0:
            z_scr[...] = z_new
        else:
            o_ref[0] = z_new.astype(o_ref.dtype)


def _hyena_call(proj, cw, cb, cbf, sbf, ar, ai, any_, skip, layer):
    bsz, seq, _ = proj.shape
    fb = min(HY_FREQ_BLOCK, seq)
    kern = functools.partial(_hyena_kernel, seq=seq, fb=fb)
    lsel = lambda b: (layer, 0, 0, 0)
    return pl.pallas_call(
        kern,
        out_shape=jax.ShapeDtypeStruct((bsz, seq, HY_WIDTH), BF16),
        grid=(bsz,),
        in_specs=[
            pl.BlockSpec((1, seq, 3 * HY_WIDTH), lambda b: (b, 0, COL_HY // (3 * HY_WIDTH))),
            _const_spec(cw.shape), _const_spec(cb.shape),
            _const_spec(cbf.shape), _const_spec(sbf.shape),
            pl.BlockSpec((1, 2, seq, HY_WIDTH), lsel, pipeline_mode=pl.Buffered(1)),
            pl.BlockSpec((1, 2, seq, HY_WIDTH), lsel, pipeline_mode=pl.Buffered(1)),
            pl.BlockSpec((1, 2, 1, HY_WIDTH), lsel, pipeline_mode=pl.Buffered(1)),
            pl.BlockSpec((1, 2, 1, HY_WIDTH), lsel, pipeline_mode=pl.Buffered(1)),
        ],
        out_specs=pl.BlockSpec((1, seq, HY_WIDTH), lambda b: (b, 0, 0)),
        scratch_shapes=[pltpu.VMEM((seq, 2 * HY_WIDTH), F32), pltpu.VMEM((seq, HY_WIDTH), F32),
                        pltpu.VMEM((seq, HY_WIDTH), BF16), pltpu.VMEM((seq, HY_WIDTH), F32)],
        compiler_params=_cparams(("parallel",), 60),
        name="hyena",
    )(proj, cw, cb, cbf, sbf, ar, ai, any_, skip)


def _lru_kernel(xr_ref, xg_ref, cw_ref, cb_ref, wg_ref, bg_ref, lam_ref, o_ref, a_scr, u_scr, *, seq):
    w = LRU_WIDTH
    rc = min(ROW_CHUNK, seq)
    lam = lam_ref[...]
    y = jnp.exp(-jnp.abs(lam))
    w1 = 1.0 + y
    log1p_y = jnp.where(w1 == 1.0, y, jnp.log(w1) * (y / jnp.where(w1 == 1.0, 1.0, w1 - 1.0)))
    softplus_neg = jnp.maximum(-lam, 0.0) + log1p_y
    load_row = lambda r, n: xr_ref[0, r:r + n, :]
    t_in = lax.broadcasted_iota(jnp.int32, (rc // SUBLANES, SUBLANES, w), 1)
    for r0 in range(0, seq, rc):
        x = xr_ref[0, r0:r0 + rc, :]
        xc = (cw_ref[0:1, :] * _chunk_shift(load_row, x, r0, 2, seq)
              + cw_ref[1:2, :] * _chunk_shift(load_row, x, r0, 1, seq)
              + cw_ref[2:3, :] * x + cw_ref[3:4, :] * _chunk_shift(load_row, x, r0, -1, seq) + cb_ref[...])
        xcb = xc.astype(BF16)
        for d in range(2):
            ca, cx = slice(d * w, (d + 1) * w), slice((2 + d) * w, (3 + d) * w)
            gate_a = _sigmoid(jnp.dot(xcb, wg_ref[:, ca], preferred_element_type=F32) + bg_ref[:, ca])
            gate_x = _sigmoid(jnp.dot(xcb, wg_ref[:, cx], preferred_element_type=F32) + bg_ref[:, cx])
            a = jnp.exp(-LRU_C * gate_a * softplus_neg[:, ca])
            u = jnp.sqrt(1.0 - a * a) * gate_x * xc
            a = a.reshape(rc // SUBLANES, SUBLANES, w)
            u = u.reshape(rc // SUBLANES, SUBLANES, w)
            s = 1
            while s < SUBLANES:
                keep = (t_in >= s) if d == 0 else (t_in < SUBLANES - s)
                sh = s if d == 0 else SUBLANES - s
                u = u + a * jnp.where(keep, pltpu.roll(u, sh, axis=1), 0.0)
                a = a * jnp.where(keep, pltpu.roll(a, sh, axis=1), 1.0)
                s *= 2
            a_scr[d, r0:r0 + rc, :] = a.reshape(rc, w)
            u_scr[d, r0:r0 + rc, :] = u.reshape(rc, w)

    ng = seq // SUBLANES

    def carry_step(i, carry):
        cf, cb = carry
        rf = pl.multiple_of(i * SUBLANES, SUBLANES)
        rb = pl.multiple_of((ng - 1 - i) * SUBLANES, SUBLANES)
        hf = u_scr[0, pl.ds(rf, SUBLANES), :] + a_scr[0, pl.ds(rf, SUBLANES), :] * cf
        hb = u_scr[1, pl.ds(rb, SUBLANES), :] + a_scr[1, pl.ds(rb, SUBLANES), :] * cb
        u_scr[0, pl.ds(rf, SUBLANES), :] = hf
        u_scr[1, pl.ds(rb, SUBLANES), :] = hb
        return hf[SUBLANES - 1:SUBLANES, :], hb[0:1, :]

    zero = jnp.zeros((1, w), F32)
    lax.fori_loop(0, ng, carry_step, (zero, zero), unroll=4)
    for r0 in range(0, seq, rc):
        rows = slice(r0, r0 + rc)
        o_ref[0, rows, :] = ((u_scr[0, rows, :] + u_scr[1, rows, :])
                             * _gelu_tanh(xg_ref[0, rows, :])).astype(o_ref.dtype)


def _lru_call(proj, cw, cb, wg, bg, lam):
    bsz, seq, _ = proj.shape
    w = LRU_WIDTH
    return pl.pallas_call(
        functools.partial(_lru_kernel, seq=seq),
        scratch_shapes=[pltpu.VMEM((2, seq, w), F32), pltpu.VMEM((2, seq, w), F32)],
        out_shape=jax.ShapeDtypeStruct((bsz, seq, w), BF16),
        grid=(bsz,),
        in_specs=[
            pl.BlockSpec((1, seq, w), lambda b: (b, 0, COL_LR // w)),
            pl.BlockSpec((1, seq, w), lambda b: (b, 0, COL_LG // w)),
            _const_spec(cw.shape), _const_spec(cb.shape), _const_spec(wg.shape),
            _const_spec(bg.shape), _const_spec(lam.shape),
        ],
        out_specs=pl.BlockSpec((1, seq, w), lambda b: (b, 0, 0)),
        compiler_params=_cparams(("parallel",), 48),
        name="rglru",
    )(proj, proj, cw, cb, wg, bg, lam)


def _rope128(x, cos, sin):
    lane = lax.broadcasted_iota(jnp.int32, x.shape, 1)
    half = MLA_ROPE // 2
    partner = jnp.where(lane < KR_LANE + half, pltpu.roll(x, LANE - half, axis=1),
                        pltpu.roll(x, half, axis=1))
    return x * cos + partner * sin


def _mla_prep_kernel(cq_ref, ckv_ref, kr_ref, qg_ref, wq_ref, kg_ref, wk_ref, wvt_ref, cos_ref,
                     sin_ref, q_ref, k_ref, vt_ref, *, qscale):
    cos = cos_ref[...]
    sin = sin_ref[...]
    qn = _rms_norm(cq_ref[0], qg_ref[...]).astype(BF16)
    q = jnp.dot(qn, wq_ref[...], preferred_element_type=F32)
    kvn = _rms_norm(ckv_ref[0], kg_ref[...]).astype(BF16)
    kn = jnp.dot(kvn, wk_ref[...], preferred_element_type=F32)
    v_t = lax.dot_general(wvt_ref[...], kvn, (((1,), (1,)), ((), ())), preferred_element_type=F32)
    row = lax.broadcasted_iota(jnp.int32, v_t.shape, 0)
    vt_ref[0] = jnp.where((row & (HEAD_PAD - 1)) == MLA_V, 1.0, v_t).astype(BF16)
    kr = _rope128(kr_ref[0], cos, sin)
    for h in range(MLA_HEADS):
        cols = slice(h * HEAD_PAD, (h + 1) * HEAD_PAD)
        q_ref[0, :, cols] = (_rope128(q[:, cols], cos, sin) * qscale).astype(BF16)
        k_ref[0, :, cols] = (kn[:, cols] + kr).astype(BF16)


def _mla_prep_call(proj, qg, wq, kg, wk, wvt, cos_t, sin_t):
    bsz, seq, _ = proj.shape
    hp = MLA_HEADS * HEAD_PAD
    tr = min(512, seq)
    qscale = float((MLA_NOPE + MLA_ROPE) ** -0.5 * math.log2(math.e))
    return pl.pallas_call(
        functools.partial(_mla_prep_kernel, qscale=qscale),
        out_shape=(jax.ShapeDtypeStruct((bsz, seq, hp), BF16),
                   jax.ShapeDtypeStruct((bsz, seq, hp), BF16),
                   jax.ShapeDtypeStruct((bsz, hp, seq), BF16)),
        grid=(bsz, seq // tr),
        in_specs=[
            pl.BlockSpec((1, tr, MLA_Q_LORA), lambda b, i: (b, i, COL_CQ // MLA_Q_LORA)),
            pl.BlockSpec((1, tr, MLA_KV_LORA), lambda b, i: (b, i, COL_CKV // MLA_KV_LORA)),
            pl.BlockSpec((1, tr, LANE), lambda b, i: (b, i, COL_KR // LANE)),
            _const_spec(qg.shape), _const_spec(wq.shape), _const_spec(kg.shape),
            _const_spec(wk.shape), _const_spec(wvt.shape),
            pl.BlockSpec((tr, LANE), lambda b, i: (i, 0)), pl.BlockSpec((tr, LANE), lambda b, i: (i, 0)),
        ],
        out_specs=(pl.BlockSpec((1, tr, hp), lambda b, i: (b, i, 0)),
                   pl.BlockSpec((1, tr, hp), lambda b, i: (b, i, 0)),
                   pl.BlockSpec((1, hp, tr), lambda b, i: (b, 0, i))),
        compiler_params=_cparams(("parallel", "parallel"), 48),
        name="mla_prep",
    )(proj, proj, proj, qg, wq, kg, wk, wvt, cos_t, sin_t)


def _attn_kernel(q_ref, k_ref, vt_ref, o_ref, s_scr, p_scr):
    tq = q_ref.shape[1]
    ct = min(ATTN_COL_TILE, tq)
    chains = [(h, c0) for c0 in range(0, tq, ct) for h in range(2)]
    nt = (((1,), (1,)), ((), ()))

    ns = s_scr.shape[0]

    def scores(i):
        h, c0 = chains[i]
        cols = slice(h * HEAD_PAD, (h + 1) * HEAD_PAD)
        s_scr[i % ns] = lax.dot_general(k_ref[0, :, cols], q_ref[0, c0:c0 + ct, cols], nt,
                                        preferred_element_type=F32)

    def finish(i):
        h = chains[i][0]
        s_t = s_scr[i % ns]
        p_scr[i % 2] = jnp.exp2(s_t - jnp.max(s_t, axis=0, keepdims=True)).astype(BF16)
        o_t = jnp.dot(vt_ref[0, h * HEAD_PAD:h * HEAD_PAD + ATTN_V_ROWS, :], p_scr[i % 2],
                      preferred_element_type=F32)
        return o_t[0:MLA_V, :] / o_t[MLA_V:MLA_V + 1, :]

    for i in range(min(ATTN_LOOKAHEAD, len(chains))):
        scores(i)
    done = {}
    for i, (h, c0) in enumerate(chains):
        if i + ATTN_LOOKAHEAD < len(chains):
            scores(i + ATTN_LOOKAHEAD)
        done[h] = finish(i)
        if h == 1:
            o_ref[0, c0:c0 + ct, :] = jnp.concatenate([done[0], done[1]], axis=0).T.astype(o_ref.dtype)


def _attn_call(q, k, vt):
    bsz, seq, hp = q.shape
    tq = min(2048, seq)
    pw = 2 * HEAD_PAD
    return pl.pallas_call(
        _attn_kernel,
        out_shape=jax.ShapeDtypeStruct((bsz, seq, MLA_HEADS * MLA_V), BF16),
        grid=(bsz, MLA_HEADS // 2, seq // tq),
        in_specs=[pl.BlockSpec((1, tq, pw), lambda b, j, i: (b, i, j)),
                  pl.BlockSpec((1, seq, pw), lambda b, j, i: (b, 0, j)),
                  pl.BlockSpec((1, pw, seq), lambda b, j, i: (b, j, 0))],
        out_specs=pl.BlockSpec((1, tq, HEAD_PAD), lambda b, j, i: (b, i, j)),
        scratch_shapes=[pltpu.VMEM((ATTN_LOOKAHEAD + 1, seq, min(ATTN_COL_TILE, tq)), F32),
                        pltpu.VMEM((2, seq, min(ATTN_COL_TILE, tq)), BF16)],
        compiler_params=_cparams(("parallel", "parallel", "parallel"), 48),
        name="mla_attn",
    )(q, k, vt)


def _postmix_kernel(yh_ref, yl_ref, ya_ref, x_ref, gn_ref, wo_ref, g_ref, b_ref, o_ref, ob_ref):
    g0, g1 = HY_WIDTH, HY_WIDTH + LRU_WIDTH
    y = jnp.concatenate([_rms_norm(yh_ref[...].astype(F32), gn_ref[:, 0:g0]),
                         _rms_norm(yl_ref[...].astype(F32), gn_ref[:, g0:g1]),
                         _rms_norm(ya_ref[...].astype(F32), gn_ref[:, g1:])], axis=-1).astype(BF16)
    mix = jnp.dot(y, wo_ref[...], preferred_element_type=F32)
    x1 = _layer_norm(ALPHA * x_ref[...] + mix, g_ref[...], b_ref[...])
    o_ref[...] = x1
    ob_ref[...] = x1.astype(BF16)


def _postmix_call(yh, yl, ya, x2d, gn, wo, g, b, tm):
    t, d = x2d.shape
    row = lambda w: pl.BlockSpec((tm, w), lambda i: (i, 0))
    return pl.pallas_call(
        _postmix_kernel,
        out_shape=(jax.ShapeDtypeStruct((t, d), F32), jax.ShapeDtypeStruct((t, d), BF16)),
        grid=(t // tm,),
        in_specs=[row(HY_WIDTH), row(LRU_WIDTH), row(MLA_HEADS * MLA_V), row(d),
                  _const_spec(gn.shape), _const_spec(wo.shape), _const_spec(g.shape), _const_spec(b.shape)],
        out_specs=(row(d), row(d)),
        compiler_params=_cparams(("parallel",), 48),
        name="postmix",
    )(yh, yl, ya, x2d, gn, wo, g, b)


def _cumsum_lanes(x):
    n = x.shape[1]
    lane = lax.broadcasted_iota(jnp.int32, x.shape, 1)
    d = 1
    while d < n:
        x = x + jnp.where(lane >= d, pltpu.roll(x, d, axis=1), 0.0)
        d *= 2
    return x


def _select_kernel(x_ref, rw_ref, pos_ref, post_ref, gatet_ref, *, cap):
    x = x_ref[0]
    x_hi, x_lo = _split_bf16(x)
    w_hi, w_lo = _split_bf16(rw_ref[...])
    logits = (jnp.dot(x_hi, w_hi, preferred_element_type=F32)
              + (jnp.dot(x_hi, w_lo, preferred_element_type=F32)
                 + jnp.dot(x_lo, w_hi, preferred_element_type=F32)))
    lane = lax.broadcasted_iota(jnp.int32, logits.shape, 1)
    valid = lane < N_EXPERTS
    logits = jnp.where(valid, logits, -1e30)
    m = jnp.max(logits, axis=-1, keepdims=True)
    ex = jnp.where(valid, jnp.exp(logits - m), 0.0)
    aff = ex / jnp.sum(ex, axis=-1, keepdims=True)
    seq = aff.shape[0]
    aff_e = aff.T[0:N_EXPERTS, :]

    def body(_, carry):
        lo, hi = carry
        mid = lo + ((hi - lo + 1) >> 1)
        cnt = jnp.sum(jnp.where(aff_e >= pltpu.bitcast(mid, F32), 1.0, 0.0), axis=1, keepdims=True)
        ok = cnt >= float(cap)
        return jnp.where(ok, mid, lo), jnp.where(ok, hi, mid - 1)

    lo0 = jnp.zeros((N_EXPERTS, 1), jnp.int32)
    hi0 = jnp.full((N_EXPERTS, 1), 0x7F7FFFFF, jnp.int32)
    thr_bits, _ = lax.fori_loop(0, 31, body, (lo0, hi0))
    thr = pltpu.bitcast(thr_bits, F32)
    gt = aff_e > thr
    eq = aff_e == thr
    need = float(cap) - jnp.sum(jnp.where(gt, 1.0, 0.0), axis=1, keepdims=True)
    eq_rank = _cumsum_lanes(jnp.where(eq, 1.0, 0.0))
    sel = gt | (eq & (eq_rank <= need))
    pos = jnp.where(sel, _cumsum_lanes(jnp.where(sel, 1.0, 0.0)) - 1.0, -1.0)
    pos_ref[0] = pos
    fill = jnp.full((LANE - N_EXPERTS, seq), -1.0, F32)
    post_ref[0] = jnp.concatenate([pos, fill], axis=0).T
    gatet_ref[0] = jnp.concatenate([jnp.where(sel, aff_e, 0.0), fill], axis=0).T


def _select_call(x1, rw, cap):
    bsz, seq, d = x1.shape
    kern = functools.partial(_select_kernel, cap=cap)
    return pl.pallas_call(
        kern,
        out_shape=(jax.ShapeDtypeStruct((bsz, N_EXPERTS, seq), F32),
                   jax.ShapeDtypeStruct((bsz, seq, LANE), F32),
                   jax.ShapeDtypeStruct((bsz, seq, LANE), F32)),
        grid=(bsz,),
        in_specs=[pl.BlockSpec((1, seq, d), lambda b: (b, 0, 0)), _const_spec(rw.shape)],
        out_specs=(pl.BlockSpec((1, N_EXPERTS, seq), lambda b: (b, 0, 0)),
                   pl.BlockSpec((1, seq, LANE), lambda b: (b, 0, 0)),
                   pl.BlockSpec((1, seq, LANE), lambda b: (b, 0, 0))),
        compiler_params=_cparams(("parallel",), 48),
        name="moe_select",
    )(x1, rw)


def _gather_kernel(pos_ref, x_ref, o_ref, p_scr, *, cap):
    seq, d = x_ref.shape[1], x_ref.shape[2]
    slot = lax.broadcasted_iota(jnp.int32, (cap, seq), 0).astype(F32)
    for e in range(N_EXPERTS):
        p_scr[e * cap:(e + 1) * cap, :] = jnp.where(pos_ref[0, e:e + 1, :] == slot, 1.0, 0.0).astype(BF16)
    cw = min(MXU_N, d)
    for c in range(0, d, cw):
        o_ref[0, :, c:c + cw] = jnp.dot(p_scr[...], x_ref[0, :, c:c + cw],
                                        preferred_element_type=F32).astype(BF16)


def _gather_call(pos, x1b, cap):
    bsz, seq, d = x1b.shape
    return pl.pallas_call(
        functools.partial(_gather_kernel, cap=cap),
        out_shape=jax.ShapeDtypeStruct((bsz, N_EXPERTS * cap, d), BF16),
        grid=(bsz,),
        in_specs=[pl.BlockSpec((1, N_EXPERTS, seq), lambda b: (b, 0, 0)),
                  pl.BlockSpec((1, seq, d), lambda b: (b, 0, 0))],
        out_specs=pl.BlockSpec((1, N_EXPERTS * cap, d), lambda b: (b, 0, 0)),
        scratch_shapes=[pltpu.VMEM((N_EXPERTS * cap, seq), BF16)],
        compiler_params=_cparams(("parallel",), 56),
        name="moe_gather",
    )(pos, x1b)


def _ffn_kernel(xe_ref, wg_ref, wu_ref, wd_ref, o_ref, acc_scr, *, rows_per_dot):
    f = pl.program_id(1)
    bsz, cap, d = xe_ref.shape
    wcast = lambda ref: ref[0, 0].astype(BF16)
    nb = rows_per_dot // cap

    @pl.when(f == 0)
    def _():
        acc_scr[...] = jnp.zeros_like(acc_scr)

    def up(b0):
        xe = xe_ref[b0:b0 + nb].reshape(nb * cap, d)
        return (jnp.dot(xe, wcast(wg_ref), preferred_element_type=F32),
                jnp.dot(xe, wcast(wu_ref), preferred_element_type=F32))

    starts = list(range(0, bsz, nb))
    nxt = up(starts[0])
    for i, b0 in enumerate(starts):
        hg, hu = nxt
        if i + 1 < len(starts):
            nxt = up(starts[i + 1])
        hid = (hg * _sigmoid(hg) * hu).astype(BF16)
        rows = slice(b0 * cap, (b0 + nb) * cap)
        acc_scr[rows, :] += jnp.dot(hid, wcast(wd_ref), preferred_element_type=F32)

    @pl.when(f == pl.num_programs(1) - 1)
    def _():
        o_ref[...] = acc_scr[...].reshape(bsz, cap, d).astype(BF16)


def _ffn_call(xe, wg, wu, wd, layer, cap):
    bsz, _, d = xe.shape
    _, ne, _, ff = wg.shape
    fcw = min(512, ff)
    rows_per_dot = cap * max(1, min(bsz, 1024 // cap))
    return pl.pallas_call(
        functools.partial(_ffn_kernel, rows_per_dot=rows_per_dot),
        out_shape=jax.ShapeDtypeStruct((bsz, ne * cap, d), BF16),
        grid=(ne, ff // fcw),
        in_specs=[pl.BlockSpec((bsz, cap, d), lambda e, f: (0, e, 0)),
                  pl.BlockSpec((1, 1, d, fcw), lambda e, f: (layer, e, 0, f)),
                  pl.BlockSpec((1, 1, d, fcw), lambda e, f: (layer, e, 0, f)),
                  pl.BlockSpec((1, 1, fcw, d), lambda e, f: (layer, e, f, 0))],
        out_specs=pl.BlockSpec((bsz, cap, d), lambda e, f: (0, e, 0)),
        scratch_shapes=[pltpu.VMEM((bsz * cap, d), F32)],
        compiler_params=_cparams(("parallel", "arbitrary"), 56),
        name="moe_ffn",
    )(xe, wg, wu, wd)


def _combine_kernel(post_ref, gatet_ref, ye_ref, x_ref, g_ref, b_ref, o_ref, *, cap):
    tr = x_ref.shape[1]
    slot = lax.broadcasted_iota(jnp.int32, (tr, cap), 1).astype(F32)
    post = post_ref[0]
    gatet = gatet_ref[0]
    acc = ALPHA * x_ref[0]
    for e in range(N_EXPERTS):
        scat = jnp.where(post[:, e:e + 1] == slot, gatet[:, e:e + 1], 0.0).astype(BF16)
        acc = acc + jnp.dot(scat, ye_ref[0, e * cap:(e + 1) * cap, :], preferred_element_type=F32)
    o_ref[0] = _layer_norm(acc, g_ref[...], b_ref[...])


def _combine_call(post, gatet, ye, x1, g, b, cap):
    bsz, seq, d = x1.shape
    tr = min(1024, seq)
    kern = functools.partial(_combine_kernel, cap=cap)
    return pl.pallas_call(
        kern,
        out_shape=jax.ShapeDtypeStruct((bsz, seq, d), F32),
        grid=(bsz, seq // tr),
        in_specs=[pl.BlockSpec((1, tr, LANE), lambda bi, i: (bi, i, 0)),
                  pl.BlockSpec((1, tr, LANE), lambda bi, i: (bi, i, 0)),
                  pl.BlockSpec((1, N_EXPERTS * cap, d), lambda bi, i: (bi, 0, 0)),
                  pl.BlockSpec((1, tr, d), lambda bi, i: (bi, i, 0)),
                  _const_spec(g.shape), _const_spec(b.shape)],
        out_specs=pl.BlockSpec((1, tr, d), lambda bi, i: (bi, i, 0)),
        compiler_params=_cparams(("parallel", "parallel"), 56),
        name="moe_combine",
    )(post, gatet, ye, x1, g, b)


def _hyena_pos_table(seq):
    t = jnp.linspace(0.0, 1.0, seq, dtype=F32)[:, None]
    bands = (HY_POS_EMB - 1) // 2
    t_idx = jnp.arange(seq, dtype=F32)[:, None]
    freqs = jnp.linspace(1e-4, bands - 1, bands, dtype=F32)[None, :]
    ang = 2.0 * math.pi * t_idx * freqs / seq
    z = jnp.concatenate([t, jnp.cos(ang), -jnp.sin(ang)], axis=-1)
    return jnp.pad(z, ((0, 0), (0, LANE - HY_POS_EMB)))


def _rope_lane_tables(seq):
    half = MLA_ROPE // 2
    inv = ROPE_THETA ** (-jnp.arange(0, MLA_ROPE, 2, dtype=F32) / MLA_ROPE)
    ang = jnp.arange(seq, dtype=F32)[:, None] * inv[None, :]
    cos, sin = jnp.cos(ang), jnp.sin(ang)
    ones = jnp.ones((seq, KR_LANE), F32)
    zeros = jnp.zeros((seq, KR_LANE), F32)
    tail = LANE - KR_LANE - MLA_ROPE
    cos_t = jnp.concatenate([ones, cos, cos, jnp.ones((seq, tail), F32)], axis=-1)
    sin_t = jnp.concatenate([zeros, -sin, sin, jnp.zeros((seq, tail), F32)], axis=-1)
    return cos_t, sin_t


def _relayout_mla(w_uq, w_ukv):
    depth = w_uq.shape[0]
    qd = MLA_NOPE + MLA_ROPE
    wq = w_uq.reshape(depth, MLA_Q_LORA, MLA_HEADS, qd)
    wq = jnp.pad(wq, ((0, 0), (0, 0), (0, 0), (0, HEAD_PAD - qd)))
    wq = wq.reshape(depth, MLA_Q_LORA, MLA_HEADS * HEAD_PAD).astype(BF16)
    wkv = w_ukv.reshape(depth, MLA_KV_LORA, MLA_HEADS, MLA_NOPE + MLA_V)
    wk = jnp.pad(wkv[..., :MLA_NOPE], ((0, 0), (0, 0), (0, 0), (0, HEAD_PAD - MLA_NOPE)))
    wk = wk.reshape(depth, MLA_KV_LORA, MLA_HEADS * HEAD_PAD).astype(BF16)
    wv = jnp.pad(wkv[..., MLA_NOPE:], ((0, 0), (0, 0), (0, 0), (0, HEAD_PAD - MLA_V)))
    wvt = jnp.swapaxes(wv.reshape(depth, MLA_KV_LORA, MLA_HEADS * HEAD_PAD), 1, 2).astype(BF16)
    return wq, wk, wvt


def _relayout_lru_gates(wa, wx, ba, bx):
    depth = wa.shape[0]
    bw = LRU_WIDTH // LRU_BLOCKS
    eye = jnp.eye(LRU_BLOCKS, dtype=wa.dtype)

    def bd(w):
        full = jnp.einsum("ldnjk,nm->ldnjmk", w, eye)
        return full.reshape(depth, 2, LRU_WIDTH, LRU_WIDTH)

    a, x = bd(wa), bd(wx)
    wg = jnp.concatenate([a[:, 0], a[:, 1], x[:, 0], x[:, 1]], axis=-1).astype(BF16)
    bg = jnp.concatenate([ba[:, 0], ba[:, 1], bx[:, 0], bx[:, 1]], axis=-1)[:, None, :]
    del bw
    return wg, bg


def kernel(x, ln_in_g, ln_in_b, w_in, hy_conv_w, hy_conv_b, hy_ffn_w1, hy_ffn_b1, hy_sin_f1,
           hy_ffn_w2, hy_ffn_b2, hy_sin_f2, hy_ffn_w3, hy_skip, lru_conv_w, lru_conv_b, lru_wa, lru_ba,
           lru_wx, lru_bx, lru_lambda, mla_q_norm_g, mla_w_uq, mla_kv_norm_g, mla_w_ukv, group_norm_g,
           w_out, ln_mix_g, ln_mix_b, router_w, exp_w_gate, exp_w_up, exp_w_down, ln_ffn_g, ln_ffn_b):
    bsz, seq, d = x.shape
    depth = w_in.shape[0]
    t = bsz * seq
    tm = min(512, t)
    cap = max(1, EC_CAPACITY_FACTOR * seq // N_EXPERTS)
    row = lambda v: v.reshape(1, -1)

    cbf, sbf = _dft_table_call(seq)
    ztab = _hyena_pos_table(seq)
    max_decay = math.log(HY_DECAY_TARGET) / HY_FAST_DECAY
    min_decay = math.log(HY_DECAY_TARGET) / HY_SLOW_DECAY
    deltas = jnp.abs(jnp.linspace(min_decay, max_decay, HY_WIDTH, dtype=F32))[None, :]
    cos_t, sin_t = _rope_lane_tables(seq)
    wq, wk, wvt = _relayout_mla(mla_w_uq, mla_w_ukv)
    wg_lru, bg_lru = _relayout_lru_gates(lru_wa, lru_wx, lru_ba, lru_bx)
    w1p = jnp.pad(hy_ffn_w1, ((0, 0), (0, LANE - HY_POS_EMB), (0, 0)))
    rw_p = jnp.pad(router_w, ((0, 0), (0, 0), (0, LANE - N_EXPERTS)))
    w_out_b = w_out.astype(BF16)

    ar, ai, any_ = _hyena_filter_call(ztab, w1p, hy_ffn_b1[:, None, :], hy_sin_f1[:, None, :], hy_ffn_w2,
                                      hy_ffn_b2[:, None, :], hy_sin_f2[:, None, :], hy_ffn_w3, deltas,
                                      cbf, sbf, seq)
    skip = hy_skip[:, :, None, :]

    xc = _ln_call(x.reshape(t, d), row(ln_in_g), row(ln_in_b), tm)
    for l in range(depth):
        proj = _proj_call(xc, w_in, l, tm).reshape(bsz, seq, PROJ_W)
        y_hy = _hyena_call(proj, hy_conv_w[l], row(hy_conv_b[l]), cbf, sbf, ar, ai, any_, skip, l)
        y_lr = _lru_call(proj, lru_conv_w[l], row(lru_conv_b[l]), wg_lru[l], bg_lru[l],
                         lru_lambda[l].reshape(1, -1))
        q, k, vt = _mla_prep_call(proj, row(mla_q_norm_g[l]), wq[l], row(mla_kv_norm_g[l]), wk[l], wvt[l],
                                  cos_t, sin_t)
        y_at = _attn_call(q, k, vt)
        x1, x1b = _postmix_call(y_hy.reshape(t, -1), y_lr.reshape(t, -1), y_at.reshape(t, -1), xc,
                                row(group_norm_g[l]), w_out_b[l], row(ln_mix_g[l]), row(ln_mix_b[l]), tm)
        x1 = x1.reshape(bsz, seq, d)
        pos, post, gatet = _select_call(x1, rw_p[l], cap)
        xe = _gather_call(pos, x1b.reshape(bsz, seq, d), cap)
        ye = _ffn_call(xe, exp_w_gate, exp_w_up, exp_w_down, l, cap)
        xc = _combine_call(post, gatet, ye, x1, row(ln_ffn_g[l]), row(ln_ffn_b[l]), cap).reshape(t, d)
    return xc.reshape(bsz, seq, d)
```
